```python
import math
import jax, jax.numpy as jnp
from jax import lax
import numpy as np

D_MODEL = 1024
BATCH = 2
SEQ = 8192
DEPTH = 1

HEAD_DIM = 64
N_HEADS = D_MODEL // HEAD_DIM
SB_HEADS = N_HEADS // 2
SWA_HEADS = N_HEADS - SB_HEADS
SWA_KV_HEADS = 2
SWA_GROUP = SWA_HEADS // SWA_KV_HEADS
WINDOW = 128
BLOCK_Q = 128
SB_W = SB_HEADS * HEAD_DIM
SWA_QW = SWA_HEADS * HEAD_DIM
SWA_KW = SWA_KV_HEADS * HEAD_DIM
D_IN = 3 * SB_W + SWA_QW + 2 * SWA_KW
D_FF = ((-(-8 * D_MODEL // 3)) + 255) // 256 * 256
N_MOD = 6
DEEPNORM_ALPHA = (2.0 * DEPTH) ** 0.25
DEEPNORM_BETA = (8.0 * DEPTH) ** -0.25
LN_EPS = 1e-5
RMS_EPS = 1e-6
MASK_VALUE = -1e30

kernel_name = "hymba_stickbreak_swa_sink_deepnorm_adaln"


def layer_norm(x, g, b):
    xf = x.astype(jnp.float32)
    mu = jnp.mean(xf, axis=-1, keepdims=True)
    var = jnp.mean(jnp.square(xf - mu), axis=-1, keepdims=True)
    return ((xf - mu) * lax.rsqrt(var + LN_EPS)).astype(x.dtype) * g + b


def rms_norm(x, g):
    xf = x.astype(jnp.float32)
    ms = jnp.mean(jnp.square(xf), axis=-1, keepdims=True)
    return (xf * lax.rsqrt(ms + RMS_EPS)).astype(x.dtype) * g


def alibi_slopes(n_heads):
    return jnp.exp2(-8.0 * jnp.arange(1, n_heads + 1, dtype=jnp.float32) / n_heads)


def stick_breaking_attention(q, k, v):
    B, S, H, Dh = q.shape
    nblk = S // BLOCK_Q
    scale = 1.0 / math.sqrt(Dh)
    qb = q.reshape(B, nblk, BLOCK_Q, H, Dh).transpose(1, 0, 3, 2, 4)
    kpos = jnp.arange(S)

    def one_block(args):
        qi, i = args
        z = jnp.einsum('bhqd,bshd->bhqs', qi, k).astype(jnp.float32) * scale
        qpos = i * BLOCK_Q + jnp.arange(BLOCK_Q)
        before = kpos[None, :] < qpos[:, None]
        log_beta = jax.nn.log_sigmoid(z)
        log_rem = jnp.where(before, jax.nn.log_sigmoid(-z), 0.0)
        suffix = lax.cumsum(log_rem, axis=3, reverse=True) - log_rem
        w = jnp.where(before, jnp.exp(log_beta + suffix), 0.0)
        return jnp.einsum('bhqs,bshd->bqhd', w.astype(v.dtype), v)

    out = lax.map(one_block, (qb, jnp.arange(nblk)))
    return out.transpose(1, 0, 2, 3, 4).reshape(B, S, H * Dh)


def sliding_window_sink_attention(q, k, v, sinks):
    B, S, Hq, Dh = q.shape
    nblk = S // WINDOW
    qb = q.reshape(B, nblk, WINDOW, SWA_KV_HEADS, SWA_GROUP, Dh)

    def banded(t):
        tb = t.reshape(B, nblk, WINDOW, SWA_KV_HEADS, Dh)
        prev = jnp.pad(tb, ((0, 0), (1, 0), (0, 0), (0, 0), (0, 0)))[:, :-1]
        return jnp.concatenate([prev, tb], axis=2)

    kb, vb = banded(k), banded(v)
    s = jnp.einsum('bnqkgd,bnskd->bnkgqs', qb, kb).astype(jnp.float32) / math.sqrt(Dh)
    qi = jnp.arange(WINDOW)
    kj = jnp.arange(2 * WINDOW)
    dist = (qi[:, None] + WINDOW - kj[None, :]).astype(jnp.float32)
    in_band = (dist >= 0) & (dist < WINDOW)
    key_pos = jnp.arange(nblk)[:, None] * WINDOW - WINDOW + kj[None, :]
    mask = in_band[None, :, :] & (key_pos >= 0)[:, None, :]
    slopes = alibi_slopes(SWA_HEADS).reshape(SWA_KV_HEADS, SWA_GROUP)
    s = s - slopes[None, None, :, :, None, None] * dist[None, None, None, None]
    s = jnp.where(mask[None, :, None, None], s, MASK_VALUE)
    sink = sinks.astype(jnp.float32).reshape(SWA_KV_HEADS, SWA_GROUP)[None, None, :, :, None, None]
    m = jnp.maximum(jnp.max(s, axis=-1, keepdims=True), sink)
    p = jnp.exp(s - m)
    p = p / (jnp.sum(p, axis=-1, keepdims=True) + jnp.exp(sink - m))
    o = jnp.einsum('bnkgqs,bnskd->bnqkgd', p.astype(v.dtype), vb)
    return o.reshape(B, S, Hq * Dh)


def setup_inputs(seed: int = 0) -> dict:
    key = jax.random.key(seed)
    ks = jax.random.split(key, 16)
    f32 = jnp.float32
    nrm = lambda k, shape: jax.random.normal(k, shape, f32)
    return {
        "x": nrm(ks[0], (BATCH, SEQ, D_MODEL)),
        "c": nrm(ks[1], (BATCH, D_MODEL)),
        "w_ada": nrm(ks[2], (DEPTH, D_MODEL, N_MOD * D_MODEL)) * (0.1 * D_MODEL ** -0.5),
        "b_ada": nrm(ks[3], (DEPTH, N_MOD * D_MODEL)) * 0.01,
        "w_in": nrm(ks[4], (DEPTH, D_MODEL, D_IN)) * D_MODEL ** -0.5,
        "b_in": nrm(ks[5], (DEPTH, D_IN)) * 0.01,
        "sinks": nrm(ks[6], (DEPTH, SWA_HEADS)) * 0.5,
        "gn_sb": 1.0 + 0.01 * nrm(ks[7], (DEPTH, SB_W)),
        "gn_swa": 1.0 + 0.01 * nrm(ks[8], (DEPTH, SWA_QW)),
        "w_out": nrm(ks[9], (DEPTH, D_MODEL, D_MODEL)) * (DEEPNORM_BETA * D_MODEL ** -0.5),
        "ln1_g": 1.0 + 0.01 * nrm(ks[10], (DEPTH, D_MODEL)),
        "ln1_b": 0.01 * nrm(ks[11], (DEPTH, D_MODEL)),
        "w_gu": nrm(ks[12], (DEPTH, D_MODEL, 2 * D_FF)) * D_MODEL ** -0.5,
        "w_down": nrm(ks[13], (DEPTH, D_FF, D_MODEL)) * (DEEPNORM_BETA * D_FF ** -0.5),
        "ln2_g": 1.0 + 0.01 * nrm(ks[14], (DEPTH, D_MODEL)),
        "ln2_b": 0.01 * nrm(ks[15], (DEPTH, D_MODEL)),
    }


def reference(x, c, w_ada, b_ada, w_in, b_in, sinks, gn_sb, gn_swa, w_out,
              ln1_g, ln1_b, w_gu, w_down, ln2_g, ln2_b):
    B, S, _ = x.shape
    for l in range(DEPTH):
        mod = jax.nn.silu(c) @ w_ada[l] + b_ada[l]
        sh_a, sc_a, g_a, sh_f, sc_f, g_f = jnp.split(mod[:, None, :], N_MOD, axis=-1)

        h = x * (1.0 + sc_a) + sh_a
        proj = h @ w_in[l] + b_in[l]
        o0, o1, o2, o3, o4 = np.cumsum([SB_W, SB_W, SB_W, SWA_QW, SWA_KW])
        q_sb = proj[..., :o0].reshape(B, S, SB_HEADS, HEAD_DIM)
        k_sb = proj[..., o0:o1].reshape(B, S, SB_HEADS, HEAD_DIM)
        v_sb = proj[..., o1:o2].reshape(B, S, SB_HEADS, HEAD_DIM)
        q_sw = proj[..., o2:o3].reshape(B, S, SWA_HEADS, HEAD_DIM)
        k_sw = proj[..., o3:o4].reshape(B, S, SWA_KV_HEADS, HEAD_DIM)
        v_sw = proj[..., o4:].reshape(B, S, SWA_KV_HEADS, HEAD_DIM)

        y_sb = stick_breaking_attention(q_sb, k_sb, v_sb)
        y_sw = sliding_window_sink_attention(q_sw, k_sw, v_sw, sinks[l])
        mixed = jnp.concatenate([rms_norm(y_sb, gn_sb[l]), rms_norm(y_sw, gn_swa[l])], axis=-1)
        attn = mixed @ w_out[l]
        x = layer_norm(DEEPNORM_ALPHA * x + (1.0 + g_a) * attn, ln1_g[l], ln1_b[l])

        h = x * (1.0 + sc_f) + sh_f
        gate, up = jnp.split(h @ w_gu[l], 2, axis=-1)
        ffn = (jax.nn.silu(gate) * up) @ w_down[l]
        x = layer_norm(DEEPNORM_ALPHA * x + (1.0 + g_f) * ffn, ln2_g[l], ln2_b[l])
    return x
```

```python
import functools
import math

import jax
import jax.numpy as jnp
from jax import lax
from jax.experimental import pallas as pl
from jax.experimental.pallas import tpu as pltpu

F32 = jnp.float32
BF16 = jnp.bfloat16

HEAD_DIM = 64
SB_HEADS = 8
SWA_HEADS = 8
SWA_KV_HEADS = 2
SWA_GROUP = SWA_HEADS // SWA_KV_HEADS
WINDOW = 128
N_MOD = 6
LN_EPS = 1e-5
RMS_EPS = 1e-6
MASK_VALUE = -1e30
LANES = 128

SB_TILE = 256
SWA_TILE = 128
ROW_TILE = 512
MOD_COLS = 1536
VMEM_LIMIT = 56 * 1024 * 1024


def _layer_norm(r, g, b):
    mu = jnp.mean(r, axis=-1, keepdims=True)
    d = r - mu
    var = jnp.mean(d * d, axis=-1, keepdims=True)
    return d * lax.rsqrt(var + LN_EPS) * g + b


def _rms_norm(y, g):
    ms = jnp.mean(y * y, axis=-1, keepdims=True)
    return y * lax.rsqrt(ms + RMS_EPS) * g


def _mod_kernel(c_ref, w_ref, b_ref, o_ref):
    c = c_ref[...]
    s = c * jax.nn.sigmoid(c)
    o_ref[...] = jnp.dot(s, w_ref[...], preferred_element_type=F32,
                         precision=lax.Precision.HIGHEST) + b_ref[...]


def _mod(c, w_ada, b_ada):
    B, D = c.shape
    N = w_ada.shape[1]
    return pl.pallas_call(
        _mod_kernel,
        grid=(N // MOD_COLS,),
        in_specs=[
            pl.BlockSpec((B, D), lambda n: (0, 0)),
            pl.BlockSpec((D, MOD_COLS), lambda n: (0, n)),
            pl.BlockSpec((1, MOD_COLS), lambda n: (0, n)),
        ],
        out_specs=pl.BlockSpec((B, MOD_COLS), lambda n: (0, n)),
        out_shape=jax.ShapeDtypeStruct((B, N), F32),
        compiler_params=pltpu.CompilerParams(
            dimension_semantics=("arbitrary",), vmem_limit_bytes=VMEM_LIMIT),
        name="mod",
    )(c, w_ada, b_ada)


def _inproj_kernel(x_ref, mod_ref, w_ref, b_ref, cs_ref, o_ref, *, col_chunk):
    sh = mod_ref[0, 0:1, :]
    sc = mod_ref[0, 1:2, :]
    h = (x_ref[0] * (1.0 + sc) + sh).astype(BF16)
    n_cols = w_ref.shape[1]
    for n in range(n_cols // col_chunk):
        sl = slice(n * col_chunk, (n + 1) * col_chunk)
        p = jnp.dot(h, w_ref[:, sl], preferred_element_type=F32)
        o_ref[0, :, sl] = ((p + b_ref[:, sl]) * cs_ref[:, sl]).astype(BF16)


def _in_proj(x, mod, w_in, b_in, col_scale):
    B, S, D = x.shape
    N = w_in.shape[1]
    return pl.pallas_call(
        functools.partial(_inproj_kernel, col_chunk=768),
        grid=(B, S // ROW_TILE),
        in_specs=[
            pl.BlockSpec((1, ROW_TILE, D), lambda b, i: (b, i, 0)),
            pl.BlockSpec((1, N_MOD, D), lambda b, i: (b, 0, 0)),
            pl.BlockSpec((D, N), lambda b, i: (0, 0)),
            pl.BlockSpec((1, N), lambda b, i: (0, 0)),
            pl.BlockSpec((1, N), lambda b, i: (0, 0)),
        ],
        out_specs=pl.BlockSpec((1, ROW_TILE, N), lambda b, i: (b, i, 0)),
        out_shape=jax.ShapeDtypeStruct((B, S, N), BF16),
        compiler_params=pltpu.CompilerParams(
            dimension_semantics=("parallel", "parallel"), vmem_limit_bytes=VMEM_LIMIT),
        name="in_proj",
    )(x, mod, w_in, b_in, col_scale)


def _sb_kernel(q_ref, k_ref, v_ref, u_ref, o_ref):
    t = SB_TILE
    i = pl.program_id(2)
    q = q_ref[0]
    u = u_ref[...]
    lane = lax.broadcasted_iota(jnp.int32, (t, LANES), 1)
    row = lax.broadcasted_iota(jnp.int32, (t, t), 0)
    col = lax.broadcasted_iota(jnp.int32, (t, t), 1)
    before = col < row
    nt_dims = (((1,), (1,)), ((), ()))

    def tile(qm, j, run, acc, diag):
        start = pl.multiple_of(j * t, t)
        kj = k_ref[0, pl.ds(start, t), :]
        vj = v_ref[0, pl.ds(start, t), :]
        z = lax.dot_general(qm, kj, nt_dims, preferred_element_type=F32)
        lb = jnp.minimum(z, 0.0) - jnp.log(1.0 + jnp.exp(-jnp.abs(z)))
        lr = lb - z
        if diag:
            lr = jnp.where(before, lr, 0.0)
        hi = lr.astype(BF16)
        lo = (lr - hi.astype(F32)).astype(BF16)
        suffix = (jnp.dot(hi, u, preferred_element_type=F32)
                  + jnp.dot(lo, u, preferred_element_type=F32))
        w = jnp.exp(lb + suffix + run)
        if diag:
            w = jnp.where(before, w, 0.0)
        acc = acc + jnp.dot(w.astype(BF16), vj, preferred_element_type=F32)
        run = run + jnp.sum(lr, axis=-1, keepdims=True)
        return run, acc

    accs = []
    for half in range(2):
        in_half = (lane >= HEAD_DIM) if half else (lane < HEAD_DIM)
        qm = jnp.where(in_half, q, jnp.zeros_like(q))
        run0 = jnp.zeros((t, 1), F32)
        acc0 = jnp.zeros((t, LANES), F32)
        run, acc = tile(qm, i, run0, acc0, True)

        def body(s, carry, qm=qm):
            return tile(qm, i - 1 - s, carry[0], carry[1], False)

        run, acc = lax.fori_loop(0, i, body, (run, acc))
        accs.append(acc)
    o_ref[0] = jnp.where(lane < HEAD_DIM, accs[0], accs[1])


def _sb_attention(proj, u):
    B, S, _ = proj.shape
    t = SB_TILE
    n_pairs = SB_HEADS * HEAD_DIM // LANES
    return pl.pallas_call(
        _sb_kernel,
        grid=(B, n_pairs, S // t),
        in_specs=[
            pl.BlockSpec((1, t, LANES), lambda b, p, i: (b, i, p)),
            pl.BlockSpec((1, S, LANES), lambda b, p, i: (b, 0, n_pairs + p)),
            pl.BlockSpec((1, S, LANES), lambda b, p, i: (b, 0, 2 * n_pairs + p)),
            pl.BlockSpec((t, t), lambda b, p, i: (0, 0)),
        ],
        out_specs=pl.BlockSpec((1, t, LANES), lambda b, p, i: (b, i, p)),
        out_shape=jax.ShapeDtypeStruct((B, S, SB_HEADS * HEAD_DIM), F32),
        compiler_params=pltpu.CompilerParams(
            dimension_semantics=("parallel", "parallel", "arbitrary"),
            vmem_limit_bytes=VMEM_LIMIT),
        name="sb_attn",
    )(proj, proj, proj, u)


def _swa_kernel(sink_ref, q_ref, kp_ref, kc_ref, vp_ref, vc_ref, o_ref):
    t = SWA_TILE
    n = pl.program_id(1)
    q = q_ref[0]
    kb = jnp.concatenate([kp_ref[0], kc_ref[0]], axis=0)
    vb = jnp.concatenate([vp_ref[0], vc_ref[0]], axis=0)
    qi = lax.broadcasted_iota(jnp.int32, (t, 2 * t), 0)
    kj = lax.broadcasted_iota(jnp.int32, (t, 2 * t), 1)
    dist = qi + WINDOW - kj
    mask = (dist >= 0) & (dist < WINDOW) & ((kj >= WINDOW) | (n > 0))
    dist_f = dist.astype(F32)
    nt_dims = (((1,), (1,)), ((), ()))
    outs = []
    for h in range(SWA_HEADS):
        g = h // SWA_GROUP
        qh = q[:, h * HEAD_DIM:(h + 1) * HEAD_DIM]
        kh = kb[:, g * HEAD_DIM:(g + 1) * HEAD_DIM]
        vh = vb[:, g * HEAD_DIM:(g + 1) * HEAD_DIM]
        slope = 2.0 ** (-8.0 * (h + 1) / SWA_HEADS)
        s = lax.dot_general(qh, kh, nt_dims, preferred_element_type=F32)
        s = jnp.where(mask, s - slope * dist_f, MASK_VALUE)
        sink = sink_ref[h]
        m = jnp.maximum(jnp.max(s, axis=-1, keepdims=True), sink)
        p = jnp.exp(s - m)
        denom = jnp.sum(p, axis=-1, keepdims=True) + jnp.exp(sink - m)
        o = jnp.dot(p.astype(BF16), vh, preferred_element_type=F32)
        outs.append(o / denom)
    o_ref[0] = jnp.concatenate(outs, axis=-1)


def _swa_attention(proj, sinks):
    B, S, _ = proj.shape
    t = SWA_TILE
    q_blk = (3 * SB_HEADS * HEAD_DIM) // (SWA_HEADS * HEAD_DIM)
    k_blk = (3 * SB_HEADS + SWA_HEADS) * HEAD_DIM // LANES
    v_blk = k_blk + 1
    prev = lambda n: jnp.maximum(n - 1, 0)
    return pl.pallas_call(
        _swa_kernel,
        grid=(B, S // t),
        in_specs=[
            pl.BlockSpec(memory_space=pltpu.SMEM),
            pl.BlockSpec((1, t, SWA_HEADS * HEAD_DIM), lambda b, n: (b, n, q_blk)),
            pl.BlockSpec((1, t, LANES), lambda b, n: (b, prev(n), k_blk)),
            pl.BlockSpec((1, t, LANES), lambda b, n: (b, n, k_blk)),
            pl.BlockSpec((1, t, LANES), lambda b, n: (b, prev(n), v_blk)),
            pl.BlockSpec((1, t, LANES), lambda b, n: (b, n, v_blk)),
        ],
        out_specs=pl.BlockSpec((1, t, SWA_HEADS * HEAD_DIM), lambda b, n: (b, n, 0)),
        out_shape=jax.ShapeDtypeStruct((B, S, SWA_HEADS * HEAD_DIM), F32),
        compiler_params=pltpu.CompilerParams(
            dimension_semantics=("parallel", "arbitrary"), vmem_limit_bytes=VMEM_LIMIT),
        name="swa_attn",
    )(sinks, proj, proj, proj, proj, proj)


def _outproj_kernel(ysb_ref, ysw_ref, x_ref, mod_ref, wa_ref, wb_ref, gsb_ref, gsw_ref,
                    lg_ref, lb_ref, o_ref, *, alpha):
    n_sb = _rms_norm(ysb_ref[0], gsb_ref[...]).astype(BF16)
    n_sw = _rms_norm(ysw_ref[0], gsw_ref[...]).astype(BF16)
    attn = (jnp.dot(n_sb, wa_ref[...], preferred_element_type=F32)
            + jnp.dot(n_sw, wb_ref[...], preferred_element_type=F32))
    gate = mod_ref[0, 2:3, :]
    r = alpha * x_ref[0] + (1.0 + gate) * attn
    o_ref[0] = _layer_norm(r, lg_ref[...], lb_ref[...])


def _out_proj(y_sb, y_sw, x, mod, w_a, w_b, gn_sb, gn_swa, ln_g, ln_b, alpha):
    B, S, D = x.shape
    W = y_sb.shape[-1]
    row = lambda b, i: (b, i, 0)
    const2 = lambda b, i: (0, 0)
    return pl.pallas_call(
        functools.partial(_outproj_kernel, alpha=alpha),
        grid=(B, S // ROW_TILE),
        in_specs=[
            pl.BlockSpec((1, ROW_TILE, W), row),
            pl.BlockSpec((1, ROW_TILE, W), row),
            pl.BlockSpec((1, ROW_TILE, D), row),
            pl.BlockSpec((1, N_MOD, D), lambda b, i: (b, 0, 0)),
            pl.BlockSpec((W, D), const2),
            pl.BlockSpec((W, D), const2),
            pl.BlockSpec((1, W), const2),
            pl.BlockSpec((1, W), const2),
            pl.BlockSpec((1, D), const2),
            pl.BlockSpec((1, D), const2),
        ],
        out_specs=pl.BlockSpec((1, ROW_TILE, D), row),
        out_shape=jax.ShapeDtypeStruct((B, S, D), F32),
        compiler_params=pltpu.CompilerParams(
            dimension_semantics=("parallel", "parallel"), vmem_limit_bytes=VMEM_LIMIT),
        name="out_proj",
    )(y_sb, y_sw, x, mod, w_a, w_b, gn_sb, gn_swa, ln_g, ln_b)


def _ffn_kernel(x_ref, mod_ref, wg_ref, wu_ref, wd_ref, lg_ref, lb_ref, o_ref,
                h_ref, acc_ref, *, alpha):
    kf = pl.program_id(2)

    @pl.when(kf == 0)
    def _():
        sh = mod_ref[0, 3:4, :]
        sc = mod_ref[0, 4:5, :]
        h_ref[...] = (x_ref[0] * (1.0 + sc) + sh).astype(BF16)

    h = h_ref[...]
    gate = jnp.dot(h, wg_ref[...], preferred_element_type=F32)
    up = jnp.dot(h, wu_ref[...], preferred_element_type=F32)
    a = (gate * jax.nn.sigmoid(gate) * up).astype(BF16)
    part = jnp.dot(a, wd_ref[...], preferred_element_type=F32)

    @pl.when(kf == 0)
    def _():
        acc_ref[...] = part

    @pl.when(kf > 0)
    def _():
        acc_ref[...] += part

    @pl.when(kf == pl.num_programs(2) - 1)
    def _():
        g = mod_ref[0, 5:6, :]
        r = alpha * x_ref[0] + (1.0 + g) * acc_ref[...]
        o_ref[0] = _layer_norm(r, lg_ref[...], lb_ref[...])


def _ffn(x, mod, w_gu, w_down, ln_g, ln_b, alpha, ff_tile):
    B, S, D = x.shape
    d_ff = w_down.shape[0]
    n_ff = d_ff // ff_tile
    row = lambda b, i, k: (b, i, 0)
    const2 = lambda b, i, k: (0, 0)
    return pl.pallas_call(
        functools.partial(_ffn_kernel, alpha=alpha),
        grid=(B, S // ROW_TILE, n_ff),
        in_specs=[
            pl.BlockSpec((1, ROW_TILE, D), row),
            pl.BlockSpec((1, N_MOD, D), lambda b, i, k: (b, 0, 0)),
            pl.BlockSpec((D, ff_tile), lambda b, i, k: (0, k)),
            pl.BlockSpec((D, ff_tile), lambda b, i, k: (0, n_ff + k)),
            pl.BlockSpec((ff_tile, D), lambda b, i, k: (k, 0)),
            pl.BlockSpec((1, D), const2),
            pl.BlockSpec((1, D), const2),
        ],
        out_specs=pl.BlockSpec((1, ROW_TILE, D), row),
        out_shape=jax.ShapeDtypeStruct((B, S, D), F32),
        scratch_shapes=[pltpu.VMEM((ROW_TILE, D), BF16), pltpu.VMEM((ROW_TILE, D), F32)],
        compiler_params=pltpu.CompilerParams(
            dimension_semantics=("parallel", "parallel", "arbitrary"),
            vmem_limit_bytes=VMEM_LIMIT),
        name="ffn",
    )(x, mod, w_gu, w_gu, w_down, ln_g, ln_b)


def kernel(x, c, w_ada, b_ada, w_in, b_in, sinks, gn_sb, gn_swa, w_out, ln1_g, ln1_b,
           w_gu, w_down, ln2_g, ln2_b):
    depth = w_ada.shape[0]
    B, S, D = x.shape
    alpha = (2.0 * depth) ** 0.25
    sb_w = SB_HEADS * HEAD_DIM
    swa_qw = SWA_HEADS * HEAD_DIM
    d_in = w_in.shape[-1]
    d_ff = w_down.shape[1]
    cols = jnp.arange(d_in)
    is_q = (cols < sb_w) | ((cols >= 3 * sb_w) & (cols < 3 * sb_w + swa_qw))
    col_scale = jnp.where(is_q, 1.0 / math.sqrt(HEAD_DIM), 1.0).astype(F32)[None, :]
    tri = (jnp.arange(SB_TILE)[:, None] > jnp.arange(SB_TILE)[None, :]).astype(BF16)
    row2 = lambda a: a.reshape(1, -1)

    for l in range(depth):
        mod = _mod(c, w_ada[l], row2(b_ada[l])).reshape(B, N_MOD, D)
        proj = _in_proj(x, mod, w_in[l].astype(BF16), row2(b_in[l]), col_scale)
        y_sb = _sb_attention(proj, tri)
        y_sw = _swa_attention(proj, sinks[l])
        w_o = w_out[l].astype(BF16)
        x = _out_proj(y_sb, y_sw, x, mod, w_o[:sb_w], w_o[sb_w:], row2(gn_sb[l]),
                      row2(gn_swa[l]), row2(ln1_g[l]), row2(ln1_b[l]), alpha)
        x = _ffn(x, mod, w_gu[l].astype(BF16), w_down[l].astype(BF16), row2(ln2_g[l]),
                 row2(ln2_b[l]), alpha, d_ff // 2)
    return x
```

```python
import functools
import math

import jax
import jax.numpy as jnp
from jax import lax
from jax.experimental import pallas as pl
from jax.experimental.pallas import tpu as pltpu

F32 = jnp.float32
BF16 = jnp.bfloat16

HEAD_DIM = 64
SB_HEADS = 8
SWA_HEADS = 8
SWA_KV_HEADS = 2
SWA_GROUP = SWA_HEADS // SWA_KV_HEADS
WINDOW = 128
N_MOD = 6
LN_EPS = 1e-5
RMS_EPS = 1e-6
MASK_VALUE = -1e30
EXP_UNDERFLOW = -104.0
LANES = 128

SB_TILE = 256
SWA_TILE = 128
ROW_TILE = 512
MOD_COLS = 1536
VMEM_LIMIT = 56 * 1024 * 1024


def _layer_norm(r, g, b):
    mu = jnp.mean(r, axis=-1, keepdims=True)
    d = r - mu
    var = jnp.mean(d * d, axis=-1, keepdims=True)
    return d * lax.rsqrt(var + LN_EPS) * g + b


def _rms_norm(y, g):
    ms = jnp.mean(y * y, axis=-1, keepdims=True)
    return y * lax.rsqrt(ms + RMS_EPS) * g


def _mod_kernel(c_ref, w_ref, b_ref, o_ref):
    c = c_ref[...]
    s = c * jax.nn.sigmoid(c)
    o_ref[...] = jnp.dot(s, w_ref[...], preferred_element_type=F32,
                         precision=lax.Precision.HIGHEST) + b_ref[...]


def _mod(c, w_ada, b_ada):
    B, D = c.shape
    N = w_ada.shape[1]
    return pl.pallas_call(
        _mod_kernel,
        grid=(N // MOD_COLS,),
        in_specs=[
            pl.BlockSpec((B, D), lambda n: (0, 0)),
            pl.BlockSpec((D, MOD_COLS), lambda n: (0, n)),
            pl.BlockSpec((1, MOD_COLS), lambda n: (0, n)),
        ],
        out_specs=pl.BlockSpec((B, MOD_COLS), lambda n: (0, n)),
        out_shape=jax.ShapeDtypeStruct((B, N), F32),
        compiler_params=pltpu.CompilerParams(
            dimension_semantics=("arbitrary",), vmem_limit_bytes=VMEM_LIMIT),
        name="mod",
    )(c, w_ada, b_ada)


def _inproj_kernel(x_ref, mod_ref, w_ref, b_ref, cs_ref, o_ref, *, col_chunk):
    sh = mod_ref[0, 0:1, :]
    sc = mod_ref[0, 1:2, :]
    h = (x_ref[0] * (1.0 + sc) + sh).astype(BF16)
    n_cols = w_ref.shape[1]
    for n in range(n_cols // col_chunk):
        sl = slice(n * col_chunk, (n + 1) * col_chunk)
        p = jnp.dot(h, w_ref[:, sl], preferred_element_type=F32)
        o_ref[0, :, sl] = ((p + b_ref[:, sl]) * cs_ref[:, sl]).astype(BF16)


def _in_proj(x, mod, w_in, b_in, col_scale):
    B, S, D = x.shape
    N = w_in.shape[1]
    return pl.pallas_call(
        functools.partial(_inproj_kernel, col_chunk=768),
        grid=(B, S // ROW_TILE),
        in_specs=[
            pl.BlockSpec((1, ROW_TILE, D), lambda b, i: (b, i, 0)),
            pl.BlockSpec((1, N_MOD, D), lambda b, i: (b, 0, 0)),
            pl.BlockSpec((D, N), lambda b, i: (0, 0)),
            pl.BlockSpec((1, N), lambda b, i: (0, 0)),
            pl.BlockSpec((1, N), lambda b, i: (0, 0)),
        ],
        out_specs=pl.BlockSpec((1, ROW_TILE, N), lambda b, i: (b, i, 0)),
        out_shape=jax.ShapeDtypeStruct((B, S, N), BF16),
        compiler_params=pltpu.CompilerParams(
            dimension_semantics=("parallel", "parallel"), vmem_limit_bytes=VMEM_LIMIT),
        name="in_proj",
    )(x, mod, w_in, b_in, col_scale)


def _sb_kernel(q_ref, k_ref, v_ref, u_ref, o_ref):
    t = SB_TILE
    i = pl.program_id(2)
    q = q_ref[0]
    u = u_ref[...]
    lane = lax.broadcasted_iota(jnp.int32, (t, LANES), 1)
    first = lane < HEAD_DIM
    zero = jnp.zeros_like(q)
    qs = jnp.concatenate([jnp.where(first, q, zero), jnp.where(first, zero, q)], axis=0)
    row = lax.broadcasted_iota(jnp.int32, (2 * t, t), 0)
    col = lax.broadcasted_iota(jnp.int32, (2 * t, t), 1)
    before = col < (row & (t - 1))
    nt_dims = (((1,), (1,)), ((), ()))

    def tile(j, run, acc, diag):
        start = pl.multiple_of(j * t, t)
        kj = k_ref[0, pl.ds(start, t), :]
        vj = v_ref[0, pl.ds(start, t), :]
        z = lax.dot_general(qs, kj, nt_dims, preferred_element_type=F32)
        lb = jnp.minimum(z, 0.0) - jnp.log(1.0 + jnp.exp(-jnp.abs(z)))
        lr = lb - z
        if diag:
            lr = jnp.where(before, lr, 0.0)
        hi = lr.astype(BF16)
        lo = (lr - hi.astype(F32)).astype(BF16)
        suffix = (jnp.dot(hi, u, preferred_element_type=F32)
                  + jnp.dot(lo, u, preferred_element_type=F32))
        w = jnp.exp(lb + suffix + run)
        if diag:
            w = jnp.where(before, w, 0.0)
        w = w.astype(BF16)
        w2 = jnp.concatenate([w[:t], w[t:]], axis=1)
        v2 = jnp.concatenate([jnp.where(first, vj, zero), jnp.where(first, zero, vj)], axis=0)
        acc = acc + jnp.dot(w2, v2, preferred_element_type=F32)
        run = run + jnp.sum(lr, axis=-1, keepdims=True)
        return run, acc

    run, acc = tile(i, jnp.zeros((2 * t, 1), F32), jnp.zeros((t, LANES), F32), True)

    def cond(carry):
        j, live, _, _ = carry
        return jnp.logical_and(j >= 0, live)

    def body(carry):
        j, _, run, acc = carry
        run, acc = tile(j, run, acc, False)
        return j - 1, jnp.max(run) >= EXP_UNDERFLOW, run, acc

    _, _, _, acc = lax.while_loop(cond, body, (i - 1, jnp.max(run) >= EXP_UNDERFLOW, run, acc))
    o_ref[0] = acc


def _sb_attention(proj, u):
    B, S, _ = proj.shape
    t = SB_TILE
    n_pairs = SB_HEADS * HEAD_DIM // LANES
    return pl.pallas_call(
        _sb_kernel,
        grid=(B, n_pairs, S // t),
        in_specs=[
            pl.BlockSpec((1, t, LANES), lambda b, p, i: (b, i, p)),
            pl.BlockSpec((1, S, LANES), lambda b, p, i: (b, 0, n_pairs + p)),
            pl.BlockSpec((1, S, LANES), lambda b, p, i: (b, 0, 2 * n_pairs + p)),
            pl.BlockSpec((t, t), lambda b, p, i: (0, 0)),
        ],
        out_specs=pl.BlockSpec((1, t, LANES), lambda b, p, i: (b, i, p)),
        out_shape=jax.ShapeDtypeStruct((B, S, SB_HEADS * HEAD_DIM), F32),
        compiler_params=pltpu.CompilerParams(
            dimension_semantics=("parallel", "parallel", "arbitrary"),
            vmem_limit_bytes=VMEM_LIMIT),
        name="sb_attn",
    )(proj, proj, proj, u)


def _swa_kernel(sink_ref, q_ref, kp_ref, kc_ref, vp_ref, vc_ref, o_ref):
    t = SWA_TILE
    n = pl.program_id(1)
    q = q_ref[0]
    kb = jnp.concatenate([kp_ref[0], kc_ref[0]], axis=0)
    vb = jnp.concatenate([vp_ref[0], vc_ref[0]], axis=0)
    qi = lax.broadcasted_iota(jnp.int32, (t, 2 * t), 0)
    kj = lax.broadcasted_iota(jnp.int32, (t, 2 * t), 1)
    dist = qi + WINDOW - kj
    mask = (dist >= 0) & (dist < WINDOW) & ((kj >= WINDOW) | (n > 0))
    dist_f = dist.astype(F32)
    nt_dims = (((1,), (1,)), ((), ()))
    outs = []
    for h in range(SWA_HEADS):
        g = h // SWA_GROUP
        qh = q[:, h * HEAD_DIM:(h + 1) * HEAD_DIM]
        kh = kb[:, g * HEAD_DIM:(g + 1) * HEAD_DIM]
        vh = vb[:, g * HEAD_DIM:(g + 1) * HEAD_DIM]
        slope = 2.0 ** (-8.0 * (h + 1) / SWA_HEADS)
        s = lax.dot_general(qh, kh, nt_dims, preferred_element_type=F32)
        s = jnp.where(mask, s - slope * dist_f, MASK_VALUE)
        sink = sink_ref[h]
        m = jnp.maximum(jnp.max(s, axis=-1, keepdims=True), sink)
        p = jnp.exp(s - m)
        denom = jnp.sum(p, axis=-1, keepdims=True) + jnp.exp(sink - m)
        o = jnp.dot(p.astype(BF16), vh, preferred_element_type=F32)
        outs.append(o / denom)
    o_ref[0] = jnp.concatenate(outs, axis=-1)


def _swa_attention(proj, sinks):
    B, S, _ = proj.shape
    t = SWA_TILE
    q_blk = (3 * SB_HEADS * HEAD_DIM) // (SWA_HEADS * HEAD_DIM)
    k_blk = (3 * SB_HEADS + SWA_HEADS) * HEAD_DIM // LANES
    v_blk = k_blk + 1
    prev = lambda n: jnp.maximum(n - 1, 0)
    return pl.pallas_call(
        _swa_kernel,
        grid=(B, S // t),
        in_specs=[
            pl.BlockSpec(memory_space=pltpu.SMEM),
            pl.BlockSpec((1, t, SWA_HEADS * HEAD_DIM), lambda b, n: (b, n, q_blk)),
            pl.BlockSpec((1, t, LANES), lambda b, n: (b, prev(n), k_blk)),
            pl.BlockSpec((1, t, LANES), lambda b, n: (b, n, k_blk)),
            pl.BlockSpec((1, t, LANES), lambda b, n: (b, prev(n), v_blk)),
            pl.BlockSpec((1, t, LANES), lambda b, n: (b, n, v_blk)),
        ],
        out_specs=pl.BlockSpec((1, t, SWA_HEADS * HEAD_DIM), lambda b, n: (b, n, 0)),
        out_shape=jax.ShapeDtypeStruct((B, S, SWA_HEADS * HEAD_DIM), F32),
        compiler_params=pltpu.CompilerParams(
            dimension_semantics=("parallel", "arbitrary"), vmem_limit_bytes=VMEM_LIMIT),
        name="swa_attn",
    )(sinks, proj, proj, proj, proj, proj)


def _outproj_kernel(ysb_ref, ysw_ref, x_ref, mod_ref, wa_ref, wb_ref, gsb_ref, gsw_ref,
                    lg_ref, lb_ref, o_ref, *, alpha):
    n_sb = _rms_norm(ysb_ref[0], gsb_ref[...]).astype(BF16)
    n_sw = _rms_norm(ysw_ref[0], gsw_ref[...]).astype(BF16)
    attn = (jnp.dot(n_sb, wa_ref[...], preferred_element_type=F32)
            + jnp.dot(n_sw, wb_ref[...], preferred_element_type=F32))
    gate = mod_ref[0, 2:3, :]
    r = alpha * x_ref[0] + (1.0 + gate) * attn
    o_ref[0] = _layer_norm(r, lg_ref[...], lb_ref[...])


def _out_proj(y_sb, y_sw, x, mod, w_a, w_b, gn_sb, gn_swa, ln_g, ln_b, alpha):
    B, S, D = x.shape
    W = y_sb.shape[-1]
    row = lambda b, i: (b, i, 0)
    const2 = lambda b, i: (0, 0)
    return pl.pallas_call(
        functools.partial(_outproj_kernel, alpha=alpha),
        grid=(B, S // ROW_TILE),
        in_specs=[
            pl.BlockSpec((1, ROW_TILE, W), row),
            pl.BlockSpec((1, ROW_TILE, W), row),
            pl.BlockSpec((1, ROW_TILE, D), row),
            pl.BlockSpec((1, N_MOD, D), lambda b, i: (b, 0, 0)),
            pl.BlockSpec((W, D), const2),
            pl.BlockSpec((W, D), const2),
            pl.BlockSpec((1, W), const2),
            pl.BlockSpec((1, W), const2),
            pl.BlockSpec((1, D), const2),
            pl.BlockSpec((1, D), const2),
        ],
        out_specs=pl.BlockSpec((1, ROW_TILE, D), row),
        out_shape=jax.ShapeDtypeStruct((B, S, D), F32),
        compiler_params=pltpu.CompilerParams(
            dimension_semantics=("parallel", "parallel"), vmem_limit_bytes=VMEM_LIMIT),
        name="out_proj",
    )(y_sb, y_sw, x, mod, w_a, w_b, gn_sb, gn_swa, ln_g, ln_b)


def _ffn_kernel(x_ref, mod_ref, wg_ref, wu_ref, wd_ref, lg_ref, lb_ref, o_ref,
                h_ref, acc_ref, *, alpha):
    kf = pl.program_id(2)

    @pl.when(kf == 0)
    def _():
        sh = mod_ref[0, 3:4, :]
        sc = mod_ref[0, 4:5, :]
        h_ref[...] = (x_ref[0] * (1.0 + sc) + sh).astype(BF16)

    h = h_ref[...]
    gate = jnp.dot(h, wg_ref[...], preferred_element_type=F32)
    up = jnp.dot(h, wu_ref[...], preferred_element_type=F32)
    a = (gate * jax.nn.sigmoid(gate) * up).astype(BF16)
    part = jnp.dot(a, wd_ref[...], preferred_element_type=F32)

    @pl.when(kf == 0)
    def _():
        acc_ref[...] = part

    @pl.when(kf > 0)
    def _():
        acc_ref[...] += part

    @pl.when(kf == pl.num_programs(2) - 1)
    def _():
        g = mod_ref[0, 5:6, :]
        r = alpha * x_ref[0] + (1.0 + g) * acc_ref[...]
        o_ref[0] = _layer_norm(r, lg_ref[...], lb_ref[...])


def _ffn(x, mod, w_gu, w_down, ln_g, ln_b, alpha, ff_tile):
    B, S, D = x.shape
    d_ff = w_down.shape[0]
    n_ff = d_ff // ff_tile
    row = lambda b, i, k: (b, i, 0)
    const2 = lambda b, i, k: (0, 0)
    return pl.pallas_call(
        functools.partial(_ffn_kernel, alpha=alpha),
        grid=(B, S // ROW_TILE, n_ff),
        in_specs=[
            pl.BlockSpec((1, ROW_TILE, D), row),
            pl.BlockSpec((1, N_MOD, D), lambda b, i, k: (b, 0, 0)),
            pl.BlockSpec((D, ff_tile), lambda b, i, k: (0, k)),
            pl.BlockSpec((D, ff_tile), lambda b, i, k: (0, n_ff + k)),
            pl.BlockSpec((ff_tile, D), lambda b, i, k: (k, 0)),
            pl.BlockSpec((1, D), const2),
            pl.BlockSpec((1, D), const2),
        ],
        out_specs=pl.BlockSpec((1, ROW_TILE, D), row),
        out_shape=jax.ShapeDtypeStruct((B, S, D), F32),
        scratch_shapes=[pltpu.VMEM((ROW_TILE, D), BF16), pltpu.VMEM((ROW_TILE, D), F32)],
        compiler_params=pltpu.CompilerParams(
            dimension_semantics=("parallel", "parallel", "arbitrary"),
            vmem_limit_bytes=VMEM_LIMIT),
        name="ffn",
    )(x, mod, w_gu, w_gu, w_down, ln_g, ln_b)


def kernel(x, c, w_ada, b_ada, w_in, b_in, sinks, gn_sb, gn_swa, w_out, ln1_g, ln1_b,
           w_gu, w_down, ln2_g, ln2_b):
    depth = w_ada.shape[0]
    B, S, D = x.shape
    alpha = (2.0 * depth) ** 0.25
    sb_w = SB_HEADS * HEAD_DIM
    swa_qw = SWA_HEADS * HEAD_DIM
    d_in = w_in.shape[-1]
    d_ff = w_down.shape[1]
    cols = jnp.arange(d_in)
    is_q = (cols < sb_w) | ((cols >= 3 * sb_w) & (cols < 3 * sb_w + swa_qw))
    col_scale = jnp.where(is_q, 1.0 / math.sqrt(HEAD_DIM), 1.0).astype(F32)[None, :]
    tri = (jnp.arange(SB_TILE)[:, None] > jnp.arange(SB_TILE)[None, :]).astype(BF16)
    row2 = lambda a: a.reshape(1, -1)

    for l in range(depth):
        mod = _mod(c, w_ada[l], row2(b_ada[l])).reshape(B, N_MOD, D)
        proj = _in_proj(x, mod, w_in[l].astype(BF16), row2(b_in[l]), col_scale)
        y_sb = _sb_attention(proj, tri)
        y_sw = _swa_attention(proj, sinks[l])
        w_o = w_out[l].astype(BF16)
        x = _out_proj(y_sb, y_sw, x, mod, w_o[:sb_w], w_o[sb_w:], row2(gn_sb[l]),
                      row2(gn_swa[l]), row2(ln1_g[l]), row2(ln1_b[l]), alpha)
        x = _ffn(x, mod, w_gu[l].astype(BF16), w_down[l].astype(BF16), row2(ln2_g[l]),
                 row2(ln2_b[l]), alpha, d_ff // 2)
    return x
```

```python
import functools
import math

import jax
import jax.numpy as jnp
from jax import lax
from jax.experimental import pallas as pl
from jax.experimental.pallas import tpu as pltpu

F32 = jnp.float32
BF16 = jnp.bfloat16

HEAD_DIM = 64
SB_HEADS = 8
SWA_HEADS = 8
SWA_KV_HEADS = 2
SWA_GROUP = SWA_HEADS // SWA_KV_HEADS
WINDOW = 128
N_MOD = 6
LN_EPS = 1e-5
RMS_EPS = 1e-6
MASK_VALUE = -1e30
EXP2_UNDERFLOW = -150.0
LOG2_E = 1.4426950408889634
LANES = 128

SB_TILE = 256
SWA_TILE = 512
ROW_TILE = 512
MOD_COLS = 1536
VMEM_LIMIT = 56 * 1024 * 1024


def _layer_norm(r, g, b):
    mu = jnp.mean(r, axis=-1, keepdims=True)
    d = r - mu
    var = jnp.mean(d * d, axis=-1, keepdims=True)
    return d * lax.rsqrt(var + LN_EPS) * g + b


def _rms_norm(y, g):
    ms = jnp.mean(y * y, axis=-1, keepdims=True)
    return y * lax.rsqrt(ms + RMS_EPS) * g


def _mod_kernel(c_ref, w_ref, b_ref, o_ref):
    c = c_ref[...]
    s = c * jax.nn.sigmoid(c)
    o_ref[...] = jnp.dot(s, w_ref[...], preferred_element_type=F32,
                         precision=lax.Precision.HIGHEST) + b_ref[...]


def _mod(c, w_ada, b_ada):
    B, D = c.shape
    N = w_ada.shape[1]
    return pl.pallas_call(
        _mod_kernel,
        grid=(N // MOD_COLS,),
        in_specs=[
            pl.BlockSpec((B, D), lambda n: (0, 0)),
            pl.BlockSpec((D, MOD_COLS), lambda n: (0, n)),
            pl.BlockSpec((1, MOD_COLS), lambda n: (0, n)),
        ],
        out_specs=pl.BlockSpec((B, MOD_COLS), lambda n: (0, n)),
        out_shape=jax.ShapeDtypeStruct((B, N), F32),
        compiler_params=pltpu.CompilerParams(
            dimension_semantics=("arbitrary",), vmem_limit_bytes=VMEM_LIMIT),
        name="mod",
    )(c, w_ada, b_ada)


def _inproj_kernel(x_ref, mod_ref, w_ref, b_ref, cs_ref, o_ref, *, col_chunk):
    sh = mod_ref[0, 0:1, :]
    sc = mod_ref[0, 1:2, :]
    h = (x_ref[0] * (1.0 + sc) + sh).astype(BF16)
    n_cols = w_ref.shape[1]
    for n in range(n_cols // col_chunk):
        sl = slice(n * col_chunk, (n + 1) * col_chunk)
        p = jnp.dot(h, w_ref[:, sl], preferred_element_type=F32)
        o_ref[0, :, sl] = ((p + b_ref[:, sl]) * cs_ref[:, sl]).astype(BF16)


def _in_proj(x, mod, w_in, b_in, col_scale):
    B, S, D = x.shape
    N = w_in.shape[1]
    return pl.pallas_call(
        functools.partial(_inproj_kernel, col_chunk=768),
        grid=(B, S // ROW_TILE),
        in_specs=[
            pl.BlockSpec((1, ROW_TILE, D), lambda b, i: (b, i, 0)),
            pl.BlockSpec((1, N_MOD, D), lambda b, i: (b, 0, 0)),
            pl.BlockSpec((D, N), lambda b, i: (0, 0)),
            pl.BlockSpec((1, N), lambda b, i: (0, 0)),
            pl.BlockSpec((1, N), lambda b, i: (0, 0)),
        ],
        out_specs=pl.BlockSpec((1, ROW_TILE, N), lambda b, i: (b, i, 0)),
        out_shape=jax.ShapeDtypeStruct((B, S, N), BF16),
        compiler_params=pltpu.CompilerParams(
            dimension_semantics=("parallel", "parallel"), vmem_limit_bytes=VMEM_LIMIT),
        name="in_proj",
    )(x, mod, w_in, b_in, col_scale)


def _sb_kernel(q_ref, k_ref, v_ref, u_ref, o_ref):
    t = SB_TILE
    n_tiles = q_ref.shape[1] // t
    u2 = u_ref[...]
    first_head = lambda rows: lax.broadcasted_iota(jnp.int32, (rows, LANES), 1) < HEAD_DIM
    row = lax.broadcasted_iota(jnp.int32, (2 * t, t), 0)
    col = lax.broadcasted_iota(jnp.int32, (2 * t, t), 1)
    before = col < (row & (t - 1))
    nt_dims = (((1,), (1,)), ((), ()))

    def log_terms(qs, keys):
        z = lax.dot_general(qs, keys, nt_dims, preferred_element_type=F32)
        lb = jnp.minimum(z, 0.0) - jnp.log2(1.0 + jnp.exp2(-jnp.abs(z)))
        return lb, lb - z

    def suffix(lr):
        hi = lr.astype(BF16)
        lo = (lr - hi.astype(F32)).astype(BF16)
        return jnp.dot(jnp.concatenate([hi, lo], axis=1), u2, preferred_element_type=F32)

    def row_total(lr):
        return jnp.sum(lr, axis=-1, keepdims=True)

    def weighted_values(w, vals):
        w = w.astype(BF16)
        zero = jnp.zeros_like(vals)
        in_first = first_head(vals.shape[0])
        w2 = jnp.concatenate([w[:t], w[t:]], axis=1)
        v2 = jnp.concatenate([jnp.where(in_first, vals, zero), jnp.where(in_first, zero, vals)],
                             axis=0)
        return jnp.dot(w2, v2, preferred_element_type=F32)

    def q_tile(i, carry):
        here = pl.multiple_of(i * t, t)
        prev = pl.multiple_of(jnp.maximum(i - 1, 0) * t, t)
        q = q_ref[0, pl.ds(here, t), :]
        zero = jnp.zeros_like(q)
        in_first = first_head(t)
        qs = jnp.concatenate([jnp.where(in_first, q, zero), jnp.where(in_first, zero, q)], axis=0)

        keys = jnp.concatenate([k_ref[0, pl.ds(prev, t), :], k_ref[0, pl.ds(here, t), :]], axis=0)
        vals = jnp.concatenate([v_ref[0, pl.ds(prev, t), :], v_ref[0, pl.ds(here, t), :]], axis=0)
        lb, lr = log_terms(qs, keys)
        lr_d = jnp.where(before, lr[:, t:], 0.0)
        lr_p = lr[:, :t]
        run_p = row_total(lr_d) + jnp.where(i == 0, MASK_VALUE, 0.0)
        w_d = jnp.where(before, jnp.exp2(lb[:, t:] + suffix(lr_d)), 0.0)
        w_p = jnp.exp2(lb[:, :t] + suffix(lr_p) + run_p)
        acc = weighted_values(jnp.concatenate([w_p, w_d], axis=1), vals)
        run = run_p + row_total(lr_p)

        def cond(c):
            j, live, _, _ = c
            return jnp.logical_and(j >= 0, live)

        def body(c):
            j, _, run, acc = c
            start = pl.multiple_of(j * t, t)
            lb, lr = log_terms(qs, k_ref[0, pl.ds(start, t), :])
            w = jnp.exp2(lb + suffix(lr) + run)
            acc = acc + weighted_values(w, v_ref[0, pl.ds(start, t), :])
            run = run + row_total(lr)
            return j - 1, jnp.max(run) >= EXP2_UNDERFLOW, run, acc

        _, _, _, acc = lax.while_loop(
            cond, body, (i - 2, jnp.max(run) >= EXP2_UNDERFLOW, run, acc))
        o_ref[0, pl.ds(here, t), :] = acc
        return carry

    lax.fori_loop(0, n_tiles, q_tile, 0)


def _sb_attention(proj, u2):
    B, S, _ = proj.shape
    n_pairs = SB_HEADS * HEAD_DIM // LANES
    seq = lambda blk: pl.BlockSpec((1, S, LANES), lambda b, p: (b, 0, blk * n_pairs + p))
    return pl.pallas_call(
        _sb_kernel,
        grid=(B, n_pairs),
        in_specs=[seq(0), seq(1), seq(2), pl.BlockSpec(u2.shape, lambda b, p: (0, 0))],
        out_specs=pl.BlockSpec((1, S, LANES), lambda b, p: (b, 0, p)),
        out_shape=jax.ShapeDtypeStruct((B, S, SB_HEADS * HEAD_DIM), F32),
        compiler_params=pltpu.CompilerParams(
            dimension_semantics=("parallel", "parallel"), vmem_limit_bytes=VMEM_LIMIT),
        name="sb_attn",
    )(proj, proj, proj, u2)


def _swa_bias():
    w = WINDOW
    r = jnp.arange(2 * w)[:, None]
    c = jnp.arange(4 * w)[None, :]
    dist = (r % w) + w - (c % (2 * w))
    in_band = (dist >= 0) & (dist < w)
    groups = []
    for g in range(SWA_KV_HEADS):
        head = SWA_GROUP * g + 2 * (r // w) + (c // (2 * w))
        slope = jnp.exp2(-8.0 * (head + 1).astype(F32) / SWA_HEADS)
        groups.append(jnp.where(in_band, -slope * dist.astype(F32), MASK_VALUE))
    rest = jnp.stack(groups)
    first = jnp.where((c % (2 * w)) >= w, rest, MASK_VALUE)
    return jnp.stack([first, rest])


def _swa_kernel(sink_ref, bias_ref, q_ref, kp_ref, kc_ref, vp_ref, vc_ref, o_ref):
    w = WINDOW
    n = pl.program_id(1)
    lane = lax.broadcasted_iota(jnp.int32, (w, LANES), 1)
    lo = lane < HEAD_DIM
    nt_dims = (((1,), (1,)), ((), ()))

    def placed(prev_ref, cur_ref):
        band = jnp.concatenate([prev_ref[0], cur_ref[0]], axis=0).astype(F32)
        swapped = pltpu.roll(band, HEAD_DIM, axis=1)
        in_lo = lax.broadcasted_iota(jnp.int32, band.shape, 1) < HEAD_DIM
        return [
            (jnp.where(in_lo, band, 0.0).astype(BF16), jnp.where(in_lo, 0.0, swapped).astype(BF16)),
            (jnp.where(in_lo, swapped, 0.0).astype(BF16), jnp.where(in_lo, 0.0, band).astype(BF16)),
        ]

    k_placed = placed(kp_ref, kc_ref)
    v_placed = placed(vp_ref, vc_ref)
    variant = jnp.minimum(n, 1)
    for a in range(SWA_TILE // w):
        rows = slice(a * w, (a + 1) * w)
        band = slice(a * w, (a + 2) * w)
        for g in range(SWA_KV_HEADS):
            cols = [slice((2 * g + rb) * LANES, (2 * g + rb + 1) * LANES) for rb in range(2)]
            q2 = jnp.concatenate([q_ref[0, rows, cols[0]], q_ref[0, rows, cols[1]]], axis=0)
            k2 = jnp.concatenate([k_placed[g][0][band], k_placed[g][1][band]], axis=0)
            v2 = jnp.concatenate([v_placed[g][0][band], v_placed[g][1][band]], axis=0)
            bias = bias_ref[variant, g] if a == 0 else bias_ref[1, g]
            s = lax.dot_general(q2, k2, nt_dims, preferred_element_type=F32) + bias
            p_rows, scales = [], []
            for rb in range(2):
                p_cols, inv = [], []
                for half in range(2):
                    sink = sink_ref[SWA_GROUP * g + 2 * rb + half]
                    sq = s[rb * w:(rb + 1) * w, half * 2 * w:(half + 1) * 2 * w]
                    m = jnp.maximum(jnp.max(sq, axis=-1, keepdims=True), sink)
                    p = jnp.exp(sq - m)
                    denom = jnp.sum(p, axis=-1, keepdims=True) + jnp.exp(sink - m)
                    p_cols.append(p.astype(BF16))
                    inv.append(1.0 / denom)
                p_rows.append(jnp.concatenate(p_cols, axis=1))
                scales.append(jnp.where(lo, inv[0], inv[1]))
            o2 = jnp.dot(jnp.concatenate(p_rows, axis=0), v2, preferred_element_type=F32)
            for rb in range(2):
                o_ref[0, rows, cols[rb]] = o2[rb * w:(rb + 1) * w] * scales[rb]


def _swa_attention(proj, sinks):
    B, S, _ = proj.shape
    t = SWA_TILE
    w = WINDOW
    q_blk = (3 * SB_HEADS * HEAD_DIM) // (SWA_HEADS * HEAD_DIM)
    k_blk = (3 * SB_HEADS + SWA_HEADS) * HEAD_DIM // LANES
    v_blk = k_blk + 1
    prev = lambda n: jnp.maximum(n * (t // w) - 1, 0)
    bias = _swa_bias()
    return pl.pallas_call(
        _swa_kernel,
        grid=(B, S // t),
        in_specs=[
            pl.BlockSpec(memory_space=pltpu.SMEM),
            pl.BlockSpec(bias.shape, lambda b, n: (0, 0, 0, 0)),
            pl.BlockSpec((1, t, SWA_HEADS * HEAD_DIM), lambda b, n: (b, n, q_blk)),
            pl.BlockSpec((1, w, LANES), lambda b, n: (b, prev(n), k_blk)),
            pl.BlockSpec((1, t, LANES), lambda b, n: (b, n, k_blk)),
            pl.BlockSpec((1, w, LANES), lambda b, n: (b, prev(n), v_blk)),
            pl.BlockSpec((1, t, LANES), lambda b, n: (b, n, v_blk)),
        ],
        out_specs=pl.BlockSpec((1, t, SWA_HEADS * HEAD_DIM), lambda b, n: (b, n, 0)),
        out_shape=jax.ShapeDtypeStruct((B, S, SWA_HEADS * HEAD_DIM), F32),
        compiler_params=pltpu.CompilerParams(
            dimension_semantics=("parallel", "arbitrary"), vmem_limit_bytes=VMEM_LIMIT),
        name="swa_attn",
    )(sinks, bias, proj, proj, proj, proj, proj)


def _outproj_kernel(ysb_ref, ysw_ref, x_ref, mod_ref, wa_ref, wb_ref, gsb_ref, gsw_ref,
                    lg_ref, lb_ref, o_ref, *, alpha):
    n_sb = _rms_norm(ysb_ref[0], gsb_ref[...]).astype(BF16)
    n_sw = _rms_norm(ysw_ref[0], gsw_ref[...]).astype(BF16)
    attn = (jnp.dot(n_sb, wa_ref[...], preferred_element_type=F32)
            + jnp.dot(n_sw, wb_ref[...], preferred_element_type=F32))
    gate = mod_ref[0, 2:3, :]
    r = alpha * x_ref[0] + (1.0 + gate) * attn
    o_ref[0] = _layer_norm(r, lg_ref[...], lb_ref[...])


def _out_proj(y_sb, y_sw, x, mod, w_a, w_b, gn_sb, gn_swa, ln_g, ln_b, alpha):
    B, S, D = x.shape
    W = y_sb.shape[-1]
    row = lambda b, i: (b, i, 0)
    const2 = lambda b, i: (0, 0)
    return pl.pallas_call(
        functools.partial(_outproj_kernel, alpha=alpha),
        grid=(B, S // ROW_TILE),
        in_specs=[
            pl.BlockSpec((1, ROW_TILE, W), row),
            pl.BlockSpec((1, ROW_TILE, W), row),
            pl.BlockSpec((1, ROW_TILE, D), row),
            pl.BlockSpec((1, N_MOD, D), lambda b, i: (b, 0, 0)),
            pl.BlockSpec((W, D), const2),
            pl.BlockSpec((W, D), const2),
            pl.BlockSpec((1, W), const2),
            pl.BlockSpec((1, W), const2),
            pl.BlockSpec((1, D), const2),
            pl.BlockSpec((1, D), const2),
        ],
        out_specs=pl.BlockSpec((1, ROW_TILE, D), row),
        out_shape=jax.ShapeDtypeStruct((B, S, D), F32),
        compiler_params=pltpu.CompilerParams(
            dimension_semantics=("parallel", "parallel"), vmem_limit_bytes=VMEM_LIMIT),
        name="out_proj",
    )(y_sb, y_sw, x, mod, w_a, w_b, gn_sb, gn_swa, ln_g, ln_b)


def _ffn_kernel(x_ref, mod_ref, wg_ref, wu_ref, wd_ref, lg_ref, lb_ref, o_ref,
                h_ref, acc_ref, *, alpha):
    kf = pl.program_id(2)

    @pl.when(kf == 0)
    def _():
        sh = mod_ref[0, 3:4, :]
        sc = mod_ref[0, 4:5, :]
        h_ref[...] = (x_ref[0] * (1.0 + sc) + sh).astype(BF16)

    h = h_ref[...]
    gate = jnp.dot(h, wg_ref[...], preferred_element_type=F32)
    up = jnp.dot(h, wu_ref[...], preferred_element_type=F32)
    a = (gate * jax.nn.sigmoid(gate) * up).astype(BF16)
    part = jnp.dot(a, wd_ref[...], preferred_element_type=F32)

    @pl.when(kf == 0)
    def _():
        acc_ref[...] = part

    @pl.when(kf > 0)
    def _():
        acc_ref[...] += part

    @pl.when(kf == pl.num_programs(2) - 1)
    def _():
        g = mod_ref[0, 5:6, :]
        r = alpha * x_ref[0] + (1.0 + g) * acc_ref[...]
        o_ref[0] = _layer_norm(r, lg_ref[...], lb_ref[...])


def _ffn(x, mod, w_gu, w_down, ln_g, ln_b, alpha, ff_tile):
    B, S, D = x.shape
    d_ff = w_down.shape[0]
    n_ff = d_ff // ff_tile
    row = lambda b, i, k: (b, i, 0)
    const2 = lambda b, i, k: (0, 0)
    return pl.pallas_call(
        functools.partial(_ffn_kernel, alpha=alpha),
        grid=(B, S // ROW_TILE, n_ff),
        in_specs=[
            pl.BlockSpec((1, ROW_TILE, D), row),
            pl.BlockSpec((1, N_MOD, D), lambda b, i, k: (b, 0, 0)),
            pl.BlockSpec((D, ff_tile), lambda b, i, k: (0, k)),
            pl.BlockSpec((D, ff_tile), lambda b, i, k: (0, n_ff + k)),
            pl.BlockSpec((ff_tile, D), lambda b, i, k: (k, 0)),
            pl.BlockSpec((1, D), const2),
            pl.BlockSpec((1, D), const2),
        ],
        out_specs=pl.BlockSpec((1, ROW_TILE, D), row),
        out_shape=jax.ShapeDtypeStruct((B, S, D), F32),
        scratch_shapes=[pltpu.VMEM((ROW_TILE, D), BF16), pltpu.VMEM((ROW_TILE, D), F32)],
        compiler_params=pltpu.CompilerParams(
            dimension_semantics=("parallel", "parallel", "arbitrary"),
            vmem_limit_bytes=VMEM_LIMIT),
        name="ffn",
    )(x, mod, w_gu, w_gu, w_down, ln_g, ln_b)


def kernel(x, c, w_ada, b_ada, w_in, b_in, sinks, gn_sb, gn_swa, w_out, ln1_g, ln1_b,
           w_gu, w_down, ln2_g, ln2_b):
    depth = w_ada.shape[0]
    B, S, D = x.shape
    alpha = (2.0 * depth) ** 0.25
    sb_w = SB_HEADS * HEAD_DIM
    swa_qw = SWA_HEADS * HEAD_DIM
    d_in = w_in.shape[-1]
    d_ff = w_down.shape[1]
    cols = jnp.arange(d_in)
    qk_scale = 1.0 / math.sqrt(HEAD_DIM)
    col_scale = jnp.where(cols < sb_w, qk_scale * LOG2_E, 1.0)
    is_swa_q = (cols >= 3 * sb_w) & (cols < 3 * sb_w + swa_qw)
    col_scale = jnp.where(is_swa_q, qk_scale, col_scale).astype(F32)[None, :]
    tri = (jnp.arange(SB_TILE)[:, None] > jnp.arange(SB_TILE)[None, :]).astype(BF16)
    tri = jnp.concatenate([tri, tri], axis=0)
    row2 = lambda a: a.reshape(1, -1)

    for l in range(depth):
        mod = _mod(c, w_ada[l], row2(b_ada[l])).reshape(B, N_MOD, D)
        proj = _in_proj(x, mod, w_in[l].astype(BF16), row2(b_in[l]), col_scale)
        y_sb = _sb_attention(proj, tri)
        y_sw = _swa_attention(proj, sinks[l])
        w_o = w_out[l].astype(BF16)
        x = _out_proj(y_sb, y_sw, x, mod, w_o[:sb_w], w_o[sb_w:], row2(gn_sb[l]),
                      row2(gn_swa[l]), row2(ln1_g[l]), row2(ln1_b[l]), alpha)
        x = _ffn(x, mod, w_gu[l].astype(BF16), w_down[l].astype(BF16), row2(ln2_g[l]),
                 row2(ln2_b[l]), alpha, d_ff // 2)
    return x
```

```python
import functools
import math

import jax
import jax.numpy as jnp
from jax import lax
from jax.experimental import pallas as pl
from jax.experimental.pallas import tpu as pltpu

F32 = jnp.float32
BF16 = jnp.bfloat16

HEAD_DIM = 64
SB_HEADS = 8
SWA_HEADS = 8
SWA_KV_HEADS = 2
SWA_GROUP = SWA_HEADS // SWA_KV_HEADS
WINDOW = 128
N_MOD = 6
LN_EPS = 1e-5
RMS_EPS = 1e-6
MASK_VALUE = -1e30
EXP2_UNDERFLOW = -150.0
LOG2_E = 1.4426950408889634
LANES = 128

SB_TILE = 256
SWA_TILE = 512
ROW_TILE = 512
MXU_TILE = 256
FF_CHUNK = 768
MOD_COLS = 1536
VMEM_LIMIT = 56 * 1024 * 1024


def _layer_norm(r, g, b):
    mu = jnp.mean(r, axis=-1, keepdims=True)
    d = r - mu
    var = jnp.mean(d * d, axis=-1, keepdims=True)
    return d * lax.rsqrt(var + LN_EPS) * g + b


def _rms_norm(y, g):
    ms = jnp.mean(y * y, axis=-1, keepdims=True)
    return y * lax.rsqrt(ms + RMS_EPS) * g


def _mod_kernel(c_ref, w_ref, b_ref, o_ref):
    c = c_ref[...]
    s = c * jax.nn.sigmoid(c)
    o_ref[...] = jnp.dot(s, w_ref[...], preferred_element_type=F32,
                         precision=lax.Precision.HIGHEST) + b_ref[...]


def _mod(c, w_ada, b_ada):
    B, D = c.shape
    N = w_ada.shape[1]
    return pl.pallas_call(
        _mod_kernel,
        grid=(N // MOD_COLS,),
        in_specs=[
            pl.BlockSpec((B, D), lambda n: (0, 0)),
            pl.BlockSpec((D, MOD_COLS), lambda n: (0, n)),
            pl.BlockSpec((1, MOD_COLS), lambda n: (0, n)),
        ],
        out_specs=pl.BlockSpec((B, MOD_COLS), lambda n: (0, n)),
        out_shape=jax.ShapeDtypeStruct((B, N), F32),
        compiler_params=pltpu.CompilerParams(
            dimension_semantics=("arbitrary",), vmem_limit_bytes=VMEM_LIMIT),
        name="mod",
    )(c, w_ada, b_ada)


def _inproj_kernel(x_ref, mod_ref, w_ref, b_ref, cs_ref, o_ref, *, col_chunk):
    sh = mod_ref[0, 0:1, :]
    sc = mod_ref[0, 1:2, :]
    h = (x_ref[0] * (1.0 + sc) + sh).astype(BF16)
    n_cols = w_ref.shape[1]
    for n in range(n_cols // col_chunk):
        sl = slice(n * col_chunk, (n + 1) * col_chunk)
        p = jnp.dot(h, w_ref[:, sl], preferred_element_type=F32)
        o_ref[0, :, sl] = ((p + b_ref[:, sl]) * cs_ref[:, sl]).astype(BF16)


def _in_proj(x, mod, w_in, b_in, col_scale):
    B, S, D = x.shape
    N = w_in.shape[1]
    return pl.pallas_call(
        functools.partial(_inproj_kernel, col_chunk=768),
        grid=(B, S // ROW_TILE),
        in_specs=[
            pl.BlockSpec((1, ROW_TILE, D), lambda b, i: (b, i, 0)),
            pl.BlockSpec((1, N_MOD, D), lambda b, i: (b, 0, 0)),
            pl.BlockSpec((D, N), lambda b, i: (0, 0)),
            pl.BlockSpec((1, N), lambda b, i: (0, 0)),
            pl.BlockSpec((1, N), lambda b, i: (0, 0)),
        ],
        out_specs=pl.BlockSpec((1, ROW_TILE, N), lambda b, i: (b, i, 0)),
        out_shape=jax.ShapeDtypeStruct((B, S, N), BF16),
        compiler_params=pltpu.CompilerParams(
            dimension_semantics=("parallel", "parallel"), vmem_limit_bytes=VMEM_LIMIT),
        name="in_proj",
    )(x, mod, w_in, b_in, col_scale)


def _sb_kernel(q_ref, k_ref, v_ref, u_ref, o_ref):
    t = SB_TILE
    n_tiles = q_ref.shape[1] // t
    u2 = u_ref[...]
    first_head = lambda rows: lax.broadcasted_iota(jnp.int32, (rows, LANES), 1) < HEAD_DIM
    row = lax.broadcasted_iota(jnp.int32, (2 * t, t), 0)
    col = lax.broadcasted_iota(jnp.int32, (2 * t, t), 1)
    before = col < (row & (t - 1))
    nt_dims = (((1,), (1,)), ((), ()))

    def log_terms(qs, keys):
        z = lax.dot_general(qs, keys, nt_dims, preferred_element_type=F32)
        lb = jnp.minimum(z, 0.0) - jnp.log2(1.0 + jnp.exp2(-jnp.abs(z)))
        return lb, lb - z

    def suffix(lr):
        hi = lr.astype(BF16)
        lo = (lr - hi.astype(F32)).astype(BF16)
        return jnp.dot(jnp.concatenate([hi, lo], axis=1), u2, preferred_element_type=F32)

    def row_total(lr):
        return jnp.sum(lr, axis=-1, keepdims=True)

    def weighted_values(w, vals):
        w = w.astype(BF16)
        zero = jnp.zeros_like(vals)
        in_first = first_head(vals.shape[0])
        w2 = jnp.concatenate([w[:t], w[t:]], axis=1)
        v2 = jnp.concatenate([jnp.where(in_first, vals, zero), jnp.where(in_first, zero, vals)],
                             axis=0)
        return jnp.dot(w2, v2, preferred_element_type=F32)

    def q_tile(i, carry):
        here = pl.multiple_of(i * t, t)
        prev = pl.multiple_of(jnp.maximum(i - 1, 0) * t, t)
        q = q_ref[0, pl.ds(here, t), :]
        zero = jnp.zeros_like(q)
        in_first = first_head(t)
        qs = jnp.concatenate([jnp.where(in_first, q, zero), jnp.where(in_first, zero, q)], axis=0)

        keys = jnp.concatenate([k_ref[0, pl.ds(prev, t), :], k_ref[0, pl.ds(here, t), :]], axis=0)
        vals = jnp.concatenate([v_ref[0, pl.ds(prev, t), :], v_ref[0, pl.ds(here, t), :]], axis=0)
        lb, lr = log_terms(qs, keys)
        lr_d = jnp.where(before, lr[:, t:], 0.0)
        lr_p = lr[:, :t]
        run_p = row_total(lr_d) + jnp.where(i == 0, MASK_VALUE, 0.0)
        w_d = jnp.where(before, jnp.exp2(lb[:, t:] + suffix(lr_d)), 0.0)
        w_p = jnp.exp2(lb[:, :t] + suffix(lr_p) + run_p)
        acc = weighted_values(jnp.concatenate([w_p, w_d], axis=1), vals)
        run = run_p + row_total(lr_p)

        def cond(c):
            j, live, _, _ = c
            return jnp.logical_and(j >= 0, live)

        def body(c):
            j, _, run, acc = c
            start = pl.multiple_of(j * t, t)
            lb, lr = log_terms(qs, k_ref[0, pl.ds(start, t), :])
            w = jnp.exp2(lb + suffix(lr) + run)
            acc = acc + weighted_values(w, v_ref[0, pl.ds(start, t), :])
            run = run + row_total(lr)
            return j - 1, jnp.max(run) >= EXP2_UNDERFLOW, run, acc

        _, _, _, acc = lax.while_loop(
            cond, body, (i - 2, jnp.max(run) >= EXP2_UNDERFLOW, run, acc))
        o_ref[0, pl.ds(here, t), :] = acc
        return carry

    lax.fori_loop(0, n_tiles, q_tile, 0)


def _sb_attention(proj, u2):
    B, S, _ = proj.shape
    n_pairs = SB_HEADS * HEAD_DIM // LANES
    seq = lambda blk: pl.BlockSpec((1, S, LANES), lambda b, p: (b, 0, blk * n_pairs + p))
    return pl.pallas_call(
        _sb_kernel,
        grid=(B, n_pairs),
        in_specs=[seq(0), seq(1), seq(2), pl.BlockSpec(u2.shape, lambda b, p: (0, 0))],
        out_specs=pl.BlockSpec((1, S, LANES), lambda b, p: (b, 0, p)),
        out_shape=jax.ShapeDtypeStruct((B, S, SB_HEADS * HEAD_DIM), F32),
        compiler_params=pltpu.CompilerParams(
            dimension_semantics=("parallel", "parallel"), vmem_limit_bytes=VMEM_LIMIT),
        name="sb_attn",
    )(proj, proj, proj, u2)


def _swa_bias():
    w = WINDOW
    r = jnp.arange(2 * w)[:, None]
    c = jnp.arange(4 * w)[None, :]
    dist = (r % w) + w - (c % (2 * w))
    in_band = (dist >= 0) & (dist < w)
    groups = []
    for g in range(SWA_KV_HEADS):
        head = SWA_GROUP * g + 2 * (r // w) + (c // (2 * w))
        slope = jnp.exp2(-8.0 * (head + 1).astype(F32) / SWA_HEADS)
        groups.append(jnp.where(in_band, -slope * dist.astype(F32), MASK_VALUE))
    rest = jnp.stack(groups)
    first = jnp.where((c % (2 * w)) >= w, rest, MASK_VALUE)
    return jnp.stack([first, rest])


def _swa_kernel(sink_ref, bias_ref, q_ref, kp_ref, kc_ref, vp_ref, vc_ref, o_ref):
    w = WINDOW
    n = pl.program_id(1)
    lane = lax.broadcasted_iota(jnp.int32, (w, LANES), 1)
    lo = lane < HEAD_DIM
    nt_dims = (((1,), (1,)), ((), ()))

    def placed(prev_ref, cur_ref):
        band = jnp.concatenate([prev_ref[0], cur_ref[0]], axis=0).astype(F32)
        swapped = pltpu.roll(band, HEAD_DIM, axis=1)
        in_lo = lax.broadcasted_iota(jnp.int32, band.shape, 1) < HEAD_DIM
        return [
            (jnp.where(in_lo, band, 0.0).astype(BF16), jnp.where(in_lo, 0.0, swapped).astype(BF16)),
            (jnp.where(in_lo, swapped, 0.0).astype(BF16), jnp.where(in_lo, 0.0, band).astype(BF16)),
        ]

    k_placed = placed(kp_ref, kc_ref)
    v_placed = placed(vp_ref, vc_ref)
    variant = jnp.minimum(n, 1)
    for a in range(SWA_TILE // w):
        rows = slice(a * w, (a + 1) * w)
        band = slice(a * w, (a + 2) * w)
        for g in range(SWA_KV_HEADS):
            cols = [slice((2 * g + rb) * LANES, (2 * g + rb + 1) * LANES) for rb in range(2)]
            q2 = jnp.concatenate([q_ref[0, rows, cols[0]], q_ref[0, rows, cols[1]]], axis=0)
            k2 = jnp.concatenate([k_placed[g][0][band], k_placed[g][1][band]], axis=0)
            v2 = jnp.concatenate([v_placed[g][0][band], v_placed[g][1][band]], axis=0)
            bias = bias_ref[variant, g] if a == 0 else bias_ref[1, g]
            s = lax.dot_general(q2, k2, nt_dims, preferred_element_type=F32) + bias
            p_rows, scales = [], []
            for rb in range(2):
                p_cols, inv = [], []
                for half in range(2):
                    sink = sink_ref[SWA_GROUP * g + 2 * rb + half]
                    sq = s[rb * w:(rb + 1) * w, half * 2 * w:(half + 1) * 2 * w]
                    m = jnp.maximum(jnp.max(sq, axis=-1, keepdims=True), sink)
                    p = jnp.exp(sq - m)
                    denom = jnp.sum(p, axis=-1, keepdims=True) + jnp.exp(sink - m)
                    p_cols.append(p.astype(BF16))
                    inv.append(1.0 / denom)
                p_rows.append(jnp.concatenate(p_cols, axis=1))
                scales.append(jnp.where(lo, inv[0], inv[1]))
            o2 = jnp.dot(jnp.concatenate(p_rows, axis=0), v2, preferred_element_type=F32)
            for rb in range(2):
                o_ref[0, rows, cols[rb]] = o2[rb * w:(rb + 1) * w] * scales[rb]


def _swa_attention(proj, sinks):
    B, S, _ = proj.shape
    t = SWA_TILE
    w = WINDOW
    q_blk = (3 * SB_HEADS * HEAD_DIM) // (SWA_HEADS * HEAD_DIM)
    k_blk = (3 * SB_HEADS + SWA_HEADS) * HEAD_DIM // LANES
    v_blk = k_blk + 1
    prev = lambda n: jnp.maximum(n * (t // w) - 1, 0)
    bias = _swa_bias()
    return pl.pallas_call(
        _swa_kernel,
        grid=(B, S // t),
        in_specs=[
            pl.BlockSpec(memory_space=pltpu.SMEM),
            pl.BlockSpec(bias.shape, lambda b, n: (0, 0, 0, 0)),
            pl.BlockSpec((1, t, SWA_HEADS * HEAD_DIM), lambda b, n: (b, n, q_blk)),
            pl.BlockSpec((1, w, LANES), lambda b, n: (b, prev(n), k_blk)),
            pl.BlockSpec((1, t, LANES), lambda b, n: (b, n, k_blk)),
            pl.BlockSpec((1, w, LANES), lambda b, n: (b, prev(n), v_blk)),
            pl.BlockSpec((1, t, LANES), lambda b, n: (b, n, v_blk)),
        ],
        out_specs=pl.BlockSpec((1, t, SWA_HEADS * HEAD_DIM), lambda b, n: (b, n, 0)),
        out_shape=jax.ShapeDtypeStruct((B, S, SWA_HEADS * HEAD_DIM), F32),
        compiler_params=pltpu.CompilerParams(
            dimension_semantics=("parallel", "arbitrary"), vmem_limit_bytes=VMEM_LIMIT),
        name="swa_attn",
    )(sinks, bias, proj, proj, proj, proj, proj)


def _outproj_kernel(ysb_ref, ysw_ref, x_ref, mod_ref, wa_ref, wb_ref, gsb_ref, gsw_ref,
                    lg_ref, lb_ref, o_ref, *, alpha):
    n_sb = _rms_norm(ysb_ref[0], gsb_ref[...]).astype(BF16)
    n_sw = _rms_norm(ysw_ref[0], gsw_ref[...]).astype(BF16)
    attn = (jnp.dot(n_sb, wa_ref[...], preferred_element_type=F32)
            + jnp.dot(n_sw, wb_ref[...], preferred_element_type=F32))
    gate = mod_ref[0, 2:3, :]
    r = alpha * x_ref[0] + (1.0 + gate) * attn
    o_ref[0] = _layer_norm(r, lg_ref[...], lb_ref[...])


def _out_proj(y_sb, y_sw, x, mod, w_a, w_b, gn_sb, gn_swa, ln_g, ln_b, alpha):
    B, S, D = x.shape
    W = y_sb.shape[-1]
    row = lambda b, i: (b, i, 0)
    const2 = lambda b, i: (0, 0)
    return pl.pallas_call(
        functools.partial(_outproj_kernel, alpha=alpha),
        grid=(B, S // ROW_TILE),
        in_specs=[
            pl.BlockSpec((1, ROW_TILE, W), row),
            pl.BlockSpec((1, ROW_TILE, W), row),
            pl.BlockSpec((1, ROW_TILE, D), row),
            pl.BlockSpec((1, N_MOD, D), lambda b, i: (b, 0, 0)),
            pl.BlockSpec((W, D), const2),
            pl.BlockSpec((W, D), const2),
            pl.BlockSpec((1, W), const2),
            pl.BlockSpec((1, W), const2),
            pl.BlockSpec((1, D), const2),
            pl.BlockSpec((1, D), const2),
        ],
        out_specs=pl.BlockSpec((1, ROW_TILE, D), row),
        out_shape=jax.ShapeDtypeStruct((B, S, D), F32),
        compiler_params=pltpu.CompilerParams(
            dimension_semantics=("parallel", "parallel"), vmem_limit_bytes=VMEM_LIMIT),
        name="out_proj",
    )(y_sb, y_sw, x, mod, w_a, w_b, gn_sb, gn_swa, ln_g, ln_b)


def _ff_chunks(d_ff):
    n_tiles = d_ff // MXU_TILE
    per = FF_CHUNK // MXU_TILE
    sizes = [per] * (n_tiles // per) + ([n_tiles % per] if n_tiles % per else [])
    return tuple(s * MXU_TILE for s in sizes)


def _ffn_kernel(x_ref, mod_ref, wgu_ref, wd_ref, lg_ref, lb_ref, o_ref, *, alpha, chunks):
    d_ff = wd_ref.shape[0]
    x = x_ref[0]
    sh = mod_ref[0, 3:4, :]
    sc = mod_ref[0, 4:5, :]
    h = (x * (1.0 + sc) + sh).astype(BF16)
    ffn = None
    start = 0
    for width in chunks:
        gate = jnp.dot(h, wgu_ref[:, start:start + width], preferred_element_type=F32)
        up = jnp.dot(h, wgu_ref[:, d_ff + start:d_ff + start + width],
                     preferred_element_type=F32)
        a = (gate * jax.nn.sigmoid(gate) * up).astype(BF16)
        part = jnp.dot(a, wd_ref[start:start + width, :], preferred_element_type=F32)
        ffn = part if ffn is None else ffn + part
        start += width
    g = mod_ref[0, 5:6, :]
    o_ref[0] = _layer_norm(alpha * x + (1.0 + g) * ffn, lg_ref[...], lb_ref[...])


def _ffn(x, mod, w_gu, w_down, ln_g, ln_b, alpha):
    B, S, D = x.shape
    d_ff = w_down.shape[0]
    row = lambda b, i: (b, i, 0)
    const2 = lambda b, i: (0, 0)
    resident = lambda shape: pl.BlockSpec(shape, const2, pipeline_mode=pl.Buffered(1))
    return pl.pallas_call(
        functools.partial(_ffn_kernel, alpha=alpha, chunks=_ff_chunks(d_ff)),
        grid=(B, S // ROW_TILE),
        in_specs=[
            pl.BlockSpec((1, ROW_TILE, D), row),
            pl.BlockSpec((1, N_MOD, D), lambda b, i: (b, 0, 0)),
            resident(w_gu.shape),
            resident(w_down.shape),
            pl.BlockSpec((1, D), const2),
            pl.BlockSpec((1, D), const2),
        ],
        out_specs=pl.BlockSpec((1, ROW_TILE, D), row),
        out_shape=jax.ShapeDtypeStruct((B, S, D), F32),
        compiler_params=pltpu.CompilerParams(
            dimension_semantics=("parallel", "parallel"), vmem_limit_bytes=VMEM_LIMIT),
        name="ffn",
    )(x, mod, w_gu, w_down, ln_g, ln_b)


def kernel(x, c, w_ada, b_ada, w_in, b_in, sinks, gn_sb, gn_swa, w_out, ln1_g, ln1_b,
           w_gu, w_down, ln2_g, ln2_b):
    depth = w_ada.shape[0]
    B, S, D = x.shape
    alpha = (2.0 * depth) ** 0.25
    sb_w = SB_HEADS * HEAD_DIM
    swa_qw = SWA_HEADS * HEAD_DIM
    d_in = w_in.shape[-1]
    d_ff = w_down.shape[1]
    cols = jnp.arange(d_in)
    qk_scale = 1.0 / math.sqrt(HEAD_DIM)
    col_scale = jnp.where(cols < sb_w, qk_scale * LOG2_E, 1.0)
    is_swa_q = (cols >= 3 * sb_w) & (cols < 3 * sb_w + swa_qw)
    col_scale = jnp.where(is_swa_q, qk_scale, col_scale).astype(F32)[None, :]
    tri = (jnp.arange(SB_TILE)[:, None] > jnp.arange(SB_TILE)[None, :]).astype(BF16)
    tri = jnp.concatenate([tri, tri], axis=0)
    row2 = lambda a: a.reshape(1, -1)

    for l in range(depth):
        mod = _mod(c, w_ada[l], row2(b_ada[l])).reshape(B, N_MOD, D)
        proj = _in_proj(x, mod, w_in[l].astype(BF16), row2(b_in[l]), col_scale)
        y_sb = _sb_attention(proj, tri)
        y_sw = _swa_attention(proj, sinks[l])
        w_o = w_out[l].astype(BF16)
        x = _out_proj(y_sb, y_sw, x, mod, w_o[:sb_w], w_o[sb_w:], row2(gn_sb[l]),
                      row2(gn_swa[l]), row2(ln1_g[l]), row2(ln1_b[l]), alpha)
        x = _ffn(x, mod, w_gu[l].astype(BF16), w_down[l].astype(BF16), row2(ln2_g[l]),
                 row2(ln2_b[l]), alpha)
    return x
```

```python
import functools
import math

import jax
import jax.numpy as jnp
from jax import lax
from jax.experimental import pallas as pl
from jax.experimental.pallas import tpu as pltpu

F32 = jnp.float32
BF16 = jnp.bfloat16

HEAD_DIM = 64
SB_HEADS = 8
SWA_HEADS = 8
SWA_KV_HEADS = 2
SWA_GROUP = SWA_HEADS // SWA_KV_HEADS
WINDOW = 128
N_MOD = 6
LN_EPS = 1e-5
RMS_EPS = 1e-6
MASK_VALUE = -1e30
EXP2_UNDERFLOW = -150.0
LOG2_E = 1.4426950408889634
LANES = 128

SB_TILE = 256
SB_UNROLL = 2
SWA_TILE = 512
ROW_TILE = 512
MXU_TILE = 256
FF_CHUNK = 768
MOD_COLS = 1536
VMEM_LIMIT = 56 * 1024 * 1024


def _layer_norm(r, g, b):
    mu = jnp.mean(r, axis=-1, keepdims=True)
    d = r - mu
    var = jnp.mean(d * d, axis=-1, keepdims=True)
    return d * lax.rsqrt(var + LN_EPS) * g + b


def _rms_norm(y, g):
    ms = jnp.mean(y * y, axis=-1, keepdims=True)
    return y * lax.rsqrt(ms + RMS_EPS) * g


def _mod_kernel(c_ref, w_ref, b_ref, o_ref):
    c = c_ref[...]
    s = c * jax.nn.sigmoid(c)
    o_ref[...] = jnp.dot(s, w_ref[...], preferred_element_type=F32,
                         precision=lax.Precision.HIGHEST) + b_ref[...]


def _mod(c, w_ada, b_ada):
    B, D = c.shape
    N = w_ada.shape[1]
    return pl.pallas_call(
        _mod_kernel,
        grid=(N // MOD_COLS,),
        in_specs=[
            pl.BlockSpec((B, D), lambda n: (0, 0)),
            pl.BlockSpec((D, MOD_COLS), lambda n: (0, n)),
            pl.BlockSpec((1, MOD_COLS), lambda n: (0, n)),
        ],
        out_specs=pl.BlockSpec((B, MOD_COLS), lambda n: (0, n)),
        out_shape=jax.ShapeDtypeStruct((B, N), F32),
        compiler_params=pltpu.CompilerParams(
            dimension_semantics=("arbitrary",), vmem_limit_bytes=VMEM_LIMIT),
        name="mod",
    )(c, w_ada, b_ada)


def _inproj_kernel(x_ref, mod_ref, w_ref, b_ref, cs_ref, o_ref, *, col_chunk):
    sh = mod_ref[0, 0:1, :]
    sc = mod_ref[0, 1:2, :]
    h = (x_ref[0] * (1.0 + sc) + sh).astype(BF16)
    n_cols = w_ref.shape[1]
    for n in range(n_cols // col_chunk):
        sl = slice(n * col_chunk, (n + 1) * col_chunk)
        p = jnp.dot(h, w_ref[:, sl], preferred_element_type=F32)
        o_ref[0, :, sl] = ((p + b_ref[:, sl]) * cs_ref[:, sl]).astype(BF16)


def _in_proj(x, mod, w_in, b_in, col_scale):
    B, S, D = x.shape
    N = w_in.shape[1]
    return pl.pallas_call(
        functools.partial(_inproj_kernel, col_chunk=768),
        grid=(B, S // ROW_TILE),
        in_specs=[
            pl.BlockSpec((1, ROW_TILE, D), lambda b, i: (b, i, 0)),
            pl.BlockSpec((1, N_MOD, D), lambda b, i: (b, 0, 0)),
            pl.BlockSpec((D, N), lambda b, i: (0, 0)),
            pl.BlockSpec((1, N), lambda b, i: (0, 0)),
            pl.BlockSpec((1, N), lambda b, i: (0, 0)),
        ],
        out_specs=pl.BlockSpec((1, ROW_TILE, N), lambda b, i: (b, i, 0)),
        out_shape=jax.ShapeDtypeStruct((B, S, N), BF16),
        compiler_params=pltpu.CompilerParams(
            dimension_semantics=("parallel", "parallel"), vmem_limit_bytes=VMEM_LIMIT),
        name="in_proj",
    )(x, mod, w_in, b_in, col_scale)


def _sb_kernel(q_ref, k_ref, v_ref, u_ref, o_ref):
    t = SB_TILE
    n_tiles = q_ref.shape[1] // t
    u2 = u_ref[...]
    first_head = lambda rows: lax.broadcasted_iota(jnp.int32, (rows, LANES), 1) < HEAD_DIM
    row = lax.broadcasted_iota(jnp.int32, (2 * t, t), 0)
    col = lax.broadcasted_iota(jnp.int32, (2 * t, t), 1)
    before = col < (row & (t - 1))
    nt_dims = (((1,), (1,)), ((), ()))

    def log_terms(qs, keys):
        z = lax.dot_general(qs, keys, nt_dims, preferred_element_type=F32)
        neg_part = jnp.minimum(z, 0.0)
        neg_relu = neg_part - z
        lr = neg_relu - jnp.log2(1.0 + jnp.exp2(neg_part + neg_relu))
        return z, lr

    def suffix(lr):
        hi = lr.astype(BF16)
        lo = (lr - hi.astype(F32)).astype(BF16)
        return jnp.dot(jnp.concatenate([hi, lo], axis=1), u2, preferred_element_type=F32)

    def row_total(lr):
        return jnp.sum(lr, axis=-1, keepdims=True)

    def weighted_values(w, vals):
        w = w.astype(BF16)
        zero = jnp.zeros_like(vals)
        in_first = first_head(vals.shape[0])
        w2 = jnp.concatenate([w[:t], w[t:]], axis=1)
        v2 = jnp.concatenate([jnp.where(in_first, vals, zero), jnp.where(in_first, zero, vals)],
                             axis=0)
        return jnp.dot(w2, v2, preferred_element_type=F32)

    def diagonal_and_previous(i):
        here = pl.multiple_of(i * t, t)
        prev = pl.multiple_of(jnp.maximum(i - 1, 0) * t, t)
        q = q_ref[0, pl.ds(here, t), :]
        zero = jnp.zeros_like(q)
        in_first = first_head(t)
        qs = jnp.concatenate([jnp.where(in_first, q, zero), jnp.where(in_first, zero, q)], axis=0)

        keys = jnp.concatenate([k_ref[0, pl.ds(prev, t), :], k_ref[0, pl.ds(here, t), :]], axis=0)
        vals = jnp.concatenate([v_ref[0, pl.ds(prev, t), :], v_ref[0, pl.ds(here, t), :]], axis=0)
        z, lr = log_terms(qs, keys)
        lr_d = jnp.where(before, lr[:, t:], 0.0)
        lr_p = lr[:, :t]
        run_p = row_total(lr_d) + jnp.where(i == 0, MASK_VALUE, 0.0)
        w_d = jnp.where(before, jnp.exp2(z[:, t:] + suffix(lr_d)), 0.0)
        w_p = jnp.exp2(z[:, :t] + suffix(lr_p) + run_p)
        acc = weighted_values(jnp.concatenate([w_p, w_d], axis=1), vals)
        return qs, run_p + row_total(lr_p), acc

    def earlier_tiles(i, qs, run, acc):
        def cond(c):
            j, live, _, _ = c
            return jnp.logical_and(j >= 0, live)

        def body(c):
            j, _, run, acc = c
            start = pl.multiple_of(j * t, t)
            z, lr = log_terms(qs, k_ref[0, pl.ds(start, t), :])
            w = jnp.exp2(z + suffix(lr) + run)
            acc = acc + weighted_values(w, v_ref[0, pl.ds(start, t), :])
            run = run + row_total(lr)
            return j - 1, jnp.max(run) >= EXP2_UNDERFLOW, run, acc

        _, _, _, acc = lax.while_loop(
            cond, body, (i - 2, jnp.max(run) >= EXP2_UNDERFLOW, run, acc))
        o_ref[0, pl.ds(pl.multiple_of(i * t, t), t), :] = acc

    def q_tiles(g, carry):
        tiles = [g * SB_UNROLL + s for s in range(SB_UNROLL)]
        heads = [diagonal_and_previous(i) for i in tiles]
        for i, (qs, run, acc) in zip(tiles, heads):
            earlier_tiles(i, qs, run, acc)
        return carry

    lax.fori_loop(0, n_tiles // SB_UNROLL, q_tiles, 0)


def _sb_attention(proj, u2):
    B, S, _ = proj.shape
    n_pairs = SB_HEADS * HEAD_DIM // LANES
    seq = lambda blk: pl.BlockSpec((1, S, LANES), lambda b, p: (b, 0, blk * n_pairs + p))
    return pl.pallas_call(
        _sb_kernel,
        grid=(B, n_pairs),
        in_specs=[seq(0), seq(1), seq(2), pl.BlockSpec(u2.shape, lambda b, p: (0, 0))],
        out_specs=pl.BlockSpec((1, S, LANES), lambda b, p: (b, 0, p)),
        out_shape=jax.ShapeDtypeStruct((B, S, SB_HEADS * HEAD_DIM), F32),
        compiler_params=pltpu.CompilerParams(
            dimension_semantics=("parallel", "parallel"), vmem_limit_bytes=VMEM_LIMIT),
        name="sb_attn",
    )(proj, proj, proj, u2)


def _swa_bias():
    w = WINDOW
    r = jnp.arange(2 * w)[:, None]
    c = jnp.arange(4 * w)[None, :]
    dist = (r % w) + w - (c % (2 * w))
    in_band = (dist >= 0) & (dist < w)
    groups = []
    for g in range(SWA_KV_HEADS):
        head = SWA_GROUP * g + 2 * (r // w) + (c // (2 * w))
        slope = jnp.exp2(-8.0 * (head + 1).astype(F32) / SWA_HEADS)
        groups.append(jnp.where(in_band, -slope * dist.astype(F32), MASK_VALUE))
    rest = jnp.stack(groups)
    first = jnp.where((c % (2 * w)) >= w, rest, MASK_VALUE)
    return jnp.stack([first, rest])


def _swa_kernel(sink_ref, bias_ref, q_ref, kp_ref, kc_ref, vp_ref, vc_ref, o_ref):
    w = WINDOW
    n = pl.program_id(1)
    lane = lax.broadcasted_iota(jnp.int32, (w, LANES), 1)
    lo = lane < HEAD_DIM
    nt_dims = (((1,), (1,)), ((), ()))

    def placed(prev_ref, cur_ref):
        band = jnp.concatenate([prev_ref[0], cur_ref[0]], axis=0).astype(F32)
        swapped = pltpu.roll(band, HEAD_DIM, axis=1)
        in_lo = lax.broadcasted_iota(jnp.int32, band.shape, 1) < HEAD_DIM
        return [
            (jnp.where(in_lo, band, 0.0).astype(BF16), jnp.where(in_lo, 0.0, swapped).astype(BF16)),
            (jnp.where(in_lo, swapped, 0.0).astype(BF16), jnp.where(in_lo, 0.0, band).astype(BF16)),
        ]

    k_placed = placed(kp_ref, kc_ref)
    v_placed = placed(vp_ref, vc_ref)
    variant = jnp.minimum(n, 1)
    for a in range(SWA_TILE // w):
        rows = slice(a * w, (a + 1) * w)
        band = slice(a * w, (a + 2) * w)
        for g in range(SWA_KV_HEADS):
            cols = [slice((2 * g + rb) * LANES, (2 * g + rb + 1) * LANES) for rb in range(2)]
            q2 = jnp.concatenate([q_ref[0, rows, cols[0]], q_ref[0, rows, cols[1]]], axis=0)
            k2 = jnp.concatenate([k_placed[g][0][band], k_placed[g][1][band]], axis=0)
            v2 = jnp.concatenate([v_placed[g][0][band], v_placed[g][1][band]], axis=0)
            bias = bias_ref[variant, g] if a == 0 else bias_ref[1, g]
            s = lax.dot_general(q2, k2, nt_dims, preferred_element_type=F32) + bias
            p_rows, scales = [], []
            for rb in range(2):
                p_cols, inv = [], []
                for half in range(2):
                    sink = sink_ref[SWA_GROUP * g + 2 * rb + half]
                    sq = s[rb * w:(rb + 1) * w, half * 2 * w:(half + 1) * 2 * w]
                    m = jnp.maximum(jnp.max(sq, axis=-1, keepdims=True), sink)
                    p = jnp.exp(sq - m)
                    denom = jnp.sum(p, axis=-1, keepdims=True) + jnp.exp(sink - m)
                    p_cols.append(p.astype(BF16))
                    inv.append(1.0 / denom)
                p_rows.append(jnp.concatenate(p_cols, axis=1))
                scales.append(jnp.where(lo, inv[0], inv[1]))
            o2 = jnp.dot(jnp.concatenate(p_rows, axis=0), v2, preferred_element_type=F32)
            for rb in range(2):
                o_ref[0, rows, cols[rb]] = o2[rb * w:(rb + 1) * w] * scales[rb]


def _swa_attention(proj, sinks):
    B, S, _ = proj.shape
    t = SWA_TILE
    w = WINDOW
    q_blk = (3 * SB_HEADS * HEAD_DIM) // (SWA_HEADS * HEAD_DIM)
    k_blk = (3 * SB_HEADS + SWA_HEADS) * HEAD_DIM // LANES
    v_blk = k_blk + 1
    prev = lambda n: jnp.maximum(n * (t // w) - 1, 0)
    bias = _swa_bias()
    return pl.pallas_call(
        _swa_kernel,
        grid=(B, S // t),
        in_specs=[
            pl.BlockSpec(memory_space=pltpu.SMEM),
            pl.BlockSpec(bias.shape, lambda b, n: (0, 0, 0, 0)),
            pl.BlockSpec((1, t, SWA_HEADS * HEAD_DIM), lambda b, n: (b, n, q_blk)),
            pl.BlockSpec((1, w, LANES), lambda b, n: (b, prev(n), k_blk)),
            pl.BlockSpec((1, t, LANES), lambda b, n: (b, n, k_blk)),
            pl.BlockSpec((1, w, LANES), lambda b, n: (b, prev(n), v_blk)),
            pl.BlockSpec((1, t, LANES), lambda b, n: (b, n, v_blk)),
        ],
        out_specs=pl.BlockSpec((1, t, SWA_HEADS * HEAD_DIM), lambda b, n: (b, n, 0)),
        out_shape=jax.ShapeDtypeStruct((B, S, SWA_HEADS * HEAD_DIM), F32),
        compiler_params=pltpu.CompilerParams(
            dimension_semantics=("parallel", "arbitrary"), vmem_limit_bytes=VMEM_LIMIT),
        name="swa_attn",
    )(sinks, bias, proj, proj, proj, proj, proj)


def _ff_chunks(d_ff):
    n_tiles = d_ff // MXU_TILE
    per = FF_CHUNK // MXU_TILE
    sizes = [per] * (n_tiles // per) + ([n_tiles % per] if n_tiles % per else [])
    return tuple(s * MXU_TILE for s in sizes)


def _post_kernel(ysb_ref, ysw_ref, x_ref, mod_ref, wo_ref, gsb_ref, gsw_ref, l1g_ref, l1b_ref,
                 wgu_ref, wd_ref, l2g_ref, l2b_ref, o_ref, *, alpha, chunks):
    sb_w = ysb_ref.shape[-1]
    d_ff = wd_ref.shape[0]
    mod = lambda k: mod_ref[0, k:k + 1, :]
    n_sb = _rms_norm(ysb_ref[0], gsb_ref[...]).astype(BF16)
    n_sw = _rms_norm(ysw_ref[0], gsw_ref[...]).astype(BF16)
    attn = (jnp.dot(n_sb, wo_ref[:sb_w, :], preferred_element_type=F32)
            + jnp.dot(n_sw, wo_ref[sb_w:, :], preferred_element_type=F32))
    x1 = _layer_norm(alpha * x_ref[0] + (1.0 + mod(2)) * attn, l1g_ref[...], l1b_ref[...])

    h = (x1 * (1.0 + mod(4)) + mod(3)).astype(BF16)
    ffn = None
    start = 0
    for width in chunks:
        gate = jnp.dot(h, wgu_ref[:, start:start + width], preferred_element_type=F32)
        up = jnp.dot(h, wgu_ref[:, d_ff + start:d_ff + start + width],
                     preferred_element_type=F32)
        a = (gate * jax.nn.sigmoid(gate) * up).astype(BF16)
        part = jnp.dot(a, wd_ref[start:start + width, :], preferred_element_type=F32)
        ffn = part if ffn is None else ffn + part
        start += width
    o_ref[0] = _layer_norm(alpha * x1 + (1.0 + mod(5)) * ffn, l2g_ref[...], l2b_ref[...])


def _post_attention(y_sb, y_sw, x, mod, w_out, gn_sb, gn_swa, ln1_g, ln1_b, w_gu, w_down,
                    ln2_g, ln2_b, alpha):
    B, S, D = x.shape
    W = y_sb.shape[-1]
    d_ff = w_down.shape[0]
    row = lambda b, i: (b, i, 0)
    const2 = lambda b, i: (0, 0)
    resident = lambda a: pl.BlockSpec(a.shape, const2, pipeline_mode=pl.Buffered(1))
    return pl.pallas_call(
        functools.partial(_post_kernel, alpha=alpha, chunks=_ff_chunks(d_ff)),
        grid=(B, S // ROW_TILE),
        in_specs=[
            pl.BlockSpec((1, ROW_TILE, W), row),
            pl.BlockSpec((1, ROW_TILE, W), row),
            pl.BlockSpec((1, ROW_TILE, D), row),
            pl.BlockSpec((1, N_MOD, D), lambda b, i: (b, 0, 0)),
            resident(w_out), resident(gn_sb), resident(gn_swa), resident(ln1_g), resident(ln1_b),
            resident(w_gu), resident(w_down), resident(ln2_g), resident(ln2_b),
        ],
        out_specs=pl.BlockSpec((1, ROW_TILE, D), row),
        out_shape=jax.ShapeDtypeStruct((B, S, D), F32),
        compiler_params=pltpu.CompilerParams(
            dimension_semantics=("parallel", "parallel"), vmem_limit_bytes=VMEM_LIMIT),
        name="post_attn",
    )(y_sb, y_sw, x, mod, w_out, gn_sb, gn_swa, ln1_g, ln1_b, w_gu, w_down, ln2_g, ln2_b)


def kernel(x, c, w_ada, b_ada, w_in, b_in, sinks, gn_sb, gn_swa, w_out, ln1_g, ln1_b,
           w_gu, w_down, ln2_g, ln2_b):
    depth = w_ada.shape[0]
    B, S, D = x.shape
    alpha = (2.0 * depth) ** 0.25
    sb_w = SB_HEADS * HEAD_DIM
    swa_qw = SWA_HEADS * HEAD_DIM
    d_in = w_in.shape[-1]
    cols = jnp.arange(d_in)
    qk_scale = 1.0 / math.sqrt(HEAD_DIM)
    col_scale = jnp.where(cols < sb_w, qk_scale * LOG2_E, 1.0)
    is_swa_q = (cols >= 3 * sb_w) & (cols < 3 * sb_w + swa_qw)
    col_scale = jnp.where(is_swa_q, qk_scale, col_scale).astype(F32)[None, :]
    tri = (jnp.arange(SB_TILE)[:, None] >= jnp.arange(SB_TILE)[None, :]).astype(BF16)
    tri = jnp.concatenate([tri, tri], axis=0)
    row2 = lambda a: a.reshape(1, -1)

    for l in range(depth):
        mod = _mod(c, w_ada[l], row2(b_ada[l])).reshape(B, N_MOD, D)
        proj = _in_proj(x, mod, w_in[l].astype(BF16), row2(b_in[l]), col_scale)
        y_sb = _sb_attention(proj, tri)
        y_sw = _swa_attention(proj, sinks[l])
        x = _post_attention(y_sb, y_sw, x, mod, w_out[l].astype(BF16), row2(gn_sb[l]),
                            row2(gn_swa[l]), row2(ln1_g[l]), row2(ln1_b[l]),
                            w_gu[l].astype(BF16), w_down[l].astype(BF16), row2(ln2_g[l]),
                            row2(ln2_b[l]), alpha)
    return x
```

```python
import functools
import math

import jax
import jax.numpy as jnp
from jax import lax
from jax.experimental import pallas as pl
from jax.experimental.pallas import tpu as pltpu

F32 = jnp.float32
BF16 = jnp.bfloat16

HEAD_DIM = 64
SB_HEADS = 8
SWA_HEADS = 8
SWA_KV_HEADS = 2
SWA_GROUP = SWA_HEADS // SWA_KV_HEADS
WINDOW = 128
N_MOD = 6
LN_EPS = 1e-5
RMS_EPS = 1e-6
MASK_VALUE = -1e30
EXP2_UNDERFLOW = -150.0
LOG2_E = 1.4426950408889634
LANES = 128

SB_TILE = 256
SB_UNROLL = 2
SWA_TILE = 512
ROW_TILE = 512
MXU_TILE = 256
FF_CHUNK = 768
MOD_COLS = 1536
VMEM_LIMIT = 56 * 1024 * 1024


def _layer_norm(r, g, b):
    mu = jnp.mean(r, axis=-1, keepdims=True)
    d = r - mu
    var = jnp.mean(d * d, axis=-1, keepdims=True)
    return d * lax.rsqrt(var + LN_EPS) * g + b


def _rms_norm(y, g):
    ms = jnp.mean(y * y, axis=-1, keepdims=True)
    return y * lax.rsqrt(ms + RMS_EPS) * g


def _mod_kernel(c_ref, w_ref, b_ref, o_ref):
    c = c_ref[...]
    s = c * jax.nn.sigmoid(c)
    o_ref[...] = jnp.dot(s, w_ref[...], preferred_element_type=F32,
                         precision=lax.Precision.HIGHEST) + b_ref[...]


def _mod(c, w_ada, b_ada):
    B, D = c.shape
    N = w_ada.shape[1]
    return pl.pallas_call(
        _mod_kernel,
        grid=(N // MOD_COLS,),
        in_specs=[
            pl.BlockSpec((B, D), lambda n: (0, 0)),
            pl.BlockSpec((D, MOD_COLS), lambda n: (0, n)),
            pl.BlockSpec((1, MOD_COLS), lambda n: (0, n)),
        ],
        out_specs=pl.BlockSpec((B, MOD_COLS), lambda n: (0, n)),
        out_shape=jax.ShapeDtypeStruct((B, N), F32),
        compiler_params=pltpu.CompilerParams(
            dimension_semantics=("arbitrary",), vmem_limit_bytes=VMEM_LIMIT),
        name="mod",
    )(c, w_ada, b_ada)


def _inproj_kernel(x_ref, mod_ref, w_ref, b_ref, cs_ref, o_ref, *, col_chunk):
    sh = mod_ref[0, 0:1, :]
    sc = mod_ref[0, 1:2, :]
    h = (x_ref[0] * (1.0 + sc) + sh).astype(BF16)
    n_cols = w_ref.shape[1]
    for n in range(n_cols // col_chunk):
        sl = slice(n * col_chunk, (n + 1) * col_chunk)
        p = jnp.dot(h, w_ref[:, sl], preferred_element_type=F32)
        o_ref[0, :, sl] = ((p + b_ref[:, sl]) * cs_ref[:, sl]).astype(BF16)


def _in_proj(x, mod, w_in, b_in, col_scale):
    B, S, D = x.shape
    N = w_in.shape[1]
    return pl.pallas_call(
        functools.partial(_inproj_kernel, col_chunk=768),
        grid=(B, S // ROW_TILE),
        in_specs=[
            pl.BlockSpec((1, ROW_TILE, D), lambda b, i: (b, i, 0)),
            pl.BlockSpec((1, N_MOD, D), lambda b, i: (b, 0, 0)),
            pl.BlockSpec((D, N), lambda b, i: (0, 0)),
            pl.BlockSpec((1, N), lambda b, i: (0, 0)),
            pl.BlockSpec((1, N), lambda b, i: (0, 0)),
        ],
        out_specs=pl.BlockSpec((1, ROW_TILE, N), lambda b, i: (b, i, 0)),
        out_shape=jax.ShapeDtypeStruct((B, S, N), BF16),
        compiler_params=pltpu.CompilerParams(
            dimension_semantics=("parallel", "parallel"), vmem_limit_bytes=VMEM_LIMIT),
        name="in_proj",
    )(x, mod, w_in, b_in, col_scale)


def _sb_kernel(q_ref, k_ref, v_ref, u_ref, o_ref):
    t = SB_TILE
    n_tiles = q_ref.shape[1] // t
    u = u_ref[...]
    first_head = lambda rows: lax.broadcasted_iota(jnp.int32, (rows, LANES), 1) < HEAD_DIM
    row = lax.broadcasted_iota(jnp.int32, (2 * t, t), 0)
    col = lax.broadcasted_iota(jnp.int32, (2 * t, t), 1)
    before = col < (row & (t - 1))
    nt_dims = (((1,), (1,)), ((), ()))

    def log_terms(qs, keys):
        z = lax.dot_general(qs, keys, nt_dims, preferred_element_type=F32)
        neg_part = jnp.minimum(z, 0.0)
        neg_relu = neg_part - z
        lr = neg_relu - jnp.log2(1.0 + jnp.exp2(neg_part + neg_relu))
        return z, lr

    def suffix(lr):
        return jnp.dot(lr.astype(BF16), u, preferred_element_type=F32)

    def row_total(lr):
        return jnp.sum(lr, axis=-1, keepdims=True)

    def weighted_values(w, vals):
        w = w.astype(BF16)
        zero = jnp.zeros_like(vals)
        in_first = first_head(vals.shape[0])
        w2 = jnp.concatenate([w[:t], w[t:]], axis=1)
        v2 = jnp.concatenate([jnp.where(in_first, vals, zero), jnp.where(in_first, zero, vals)],
                             axis=0)
        return jnp.dot(w2, v2, preferred_element_type=F32)

    def diagonal_and_previous(i):
        here = pl.multiple_of(i * t, t)
        prev = pl.multiple_of(jnp.maximum(i - 1, 0) * t, t)
        q = q_ref[0, pl.ds(here, t), :]
        zero = jnp.zeros_like(q)
        in_first = first_head(t)
        qs = jnp.concatenate([jnp.where(in_first, q, zero), jnp.where(in_first, zero, q)], axis=0)

        keys = jnp.concatenate([k_ref[0, pl.ds(prev, t), :], k_ref[0, pl.ds(here, t), :]], axis=0)
        vals = jnp.concatenate([v_ref[0, pl.ds(prev, t), :], v_ref[0, pl.ds(here, t), :]], axis=0)
        z, lr = log_terms(qs, keys)
        lr_d = jnp.where(before, lr[:, t:], 0.0)
        lr_p = lr[:, :t]
        run_p = row_total(lr_d) + jnp.where(i == 0, MASK_VALUE, 0.0)
        w_d = jnp.where(before, jnp.exp2(z[:, t:] + suffix(lr_d)), 0.0)
        w_p = jnp.exp2(z[:, :t] + suffix(lr_p) + run_p)
        acc = weighted_values(jnp.concatenate([w_p, w_d], axis=1), vals)
        return qs, run_p + row_total(lr_p), acc

    def earlier_tiles(i, qs, run, acc):
        def cond(c):
            j, live, _, _ = c
            return jnp.logical_and(j >= 0, live)

        def body(c):
            j, _, run, acc = c
            start = pl.multiple_of(j * t, t)
            z, lr = log_terms(qs, k_ref[0, pl.ds(start, t), :])
            w = jnp.exp2(z + suffix(lr) + run)
            acc = acc + weighted_values(w, v_ref[0, pl.ds(start, t), :])
            run = run + row_total(lr)
            return j - 1, jnp.max(run) >= EXP2_UNDERFLOW, run, acc

        _, _, _, acc = lax.while_loop(
            cond, body, (i - 2, jnp.max(run) >= EXP2_UNDERFLOW, run, acc))
        o_ref[0, pl.ds(pl.multiple_of(i * t, t), t), :] = acc

    def q_tiles(g, carry):
        tiles = [g * SB_UNROLL + s for s in range(SB_UNROLL)]
        heads = [diagonal_and_previous(i) for i in tiles]
        for i, (qs, run, acc) in zip(tiles, heads):
            earlier_tiles(i, qs, run, acc)
        return carry

    lax.fori_loop(0, n_tiles // SB_UNROLL, q_tiles, 0)


def _sb_attention(proj, u):
    B, S, _ = proj.shape
    n_pairs = SB_HEADS * HEAD_DIM // LANES
    seq = lambda blk: pl.BlockSpec((1, S, LANES), lambda b, p: (b, 0, blk * n_pairs + p))
    return pl.pallas_call(
        _sb_kernel,
        grid=(B, n_pairs),
        in_specs=[seq(0), seq(1), seq(2), pl.BlockSpec(u.shape, lambda b, p: (0, 0))],
        out_specs=pl.BlockSpec((1, S, LANES), lambda b, p: (b, 0, p)),
        out_shape=jax.ShapeDtypeStruct((B, S, SB_HEADS * HEAD_DIM), F32),
        compiler_params=pltpu.CompilerParams(
            dimension_semantics=("parallel", "parallel"), vmem_limit_bytes=VMEM_LIMIT),
        name="sb_attn",
    )(proj, proj, proj, u)


def _swa_bias():
    w = WINDOW
    r = jnp.arange(2 * w)[:, None]
    c = jnp.arange(4 * w)[None, :]
    dist = (r % w) + w - (c % (2 * w))
    in_band = (dist >= 0) & (dist < w)
    groups = []
    for g in range(SWA_KV_HEADS):
        head = SWA_GROUP * g + 2 * (r // w) + (c // (2 * w))
        slope = jnp.exp2(-8.0 * (head + 1).astype(F32) / SWA_HEADS)
        groups.append(jnp.where(in_band, -slope * dist.astype(F32), MASK_VALUE))
    rest = jnp.stack(groups)
    first = jnp.where((c % (2 * w)) >= w, rest, MASK_VALUE)
    return jnp.stack([first, rest])


def _swa_kernel(sink_ref, bias_ref, q_ref, kp_ref, kc_ref, vp_ref, vc_ref, o_ref):
    w = WINDOW
    n = pl.program_id(1)
    lane = lax.broadcasted_iota(jnp.int32, (w, LANES), 1)
    lo = lane < HEAD_DIM
    nt_dims = (((1,), (1,)), ((), ()))

    def placed(prev_ref, cur_ref):
        band = jnp.concatenate([prev_ref[0], cur_ref[0]], axis=0).astype(F32)
        swapped = pltpu.roll(band, HEAD_DIM, axis=1)
        in_lo = lax.broadcasted_iota(jnp.int32, band.shape, 1) < HEAD_DIM
        return [
            (jnp.where(in_lo, band, 0.0).astype(BF16), jnp.where(in_lo, 0.0, swapped).astype(BF16)),
            (jnp.where(in_lo, swapped, 0.0).astype(BF16), jnp.where(in_lo, 0.0, band).astype(BF16)),
        ]

    k_placed = placed(kp_ref, kc_ref)
    v_placed = placed(vp_ref, vc_ref)
    variant = jnp.minimum(n, 1)
    for a in range(SWA_TILE // w):
        rows = slice(a * w, (a + 1) * w)
        band = slice(a * w, (a + 2) * w)
        for g in range(SWA_KV_HEADS):
            cols = [slice((2 * g + rb) * LANES, (2 * g + rb + 1) * LANES) for rb in range(2)]
            q2 = jnp.concatenate([q_ref[0, rows, cols[0]], q_ref[0, rows, cols[1]]], axis=0)
            k2 = jnp.concatenate([k_placed[g][0][band], k_placed[g][1][band]], axis=0)
            v2 = jnp.concatenate([v_placed[g][0][band], v_placed[g][1][band]], axis=0)
            bias = bias_ref[variant, g] if a == 0 else bias_ref[1, g]
            s = lax.dot_general(q2, k2, nt_dims, preferred_element_type=F32) + bias
            p_rows, scales = [], []
            for rb in range(2):
                p_cols, inv = [], []
                for half in range(2):
                    sink = sink_ref[SWA_GROUP * g + 2 * rb + half]
                    sq = s[rb * w:(rb + 1) * w, half * 2 * w:(half + 1) * 2 * w]
                    m = jnp.maximum(jnp.max(sq, axis=-1, keepdims=True), sink)
                    p = jnp.exp(sq - m)
                    denom = jnp.sum(p, axis=-1, keepdims=True) + jnp.exp(sink - m)
                    p_cols.append(p.astype(BF16))
                    inv.append(1.0 / denom)
                p_rows.append(jnp.concatenate(p_cols, axis=1))
                scales.append(jnp.where(lo, inv[0], inv[1]))
            o2 = jnp.dot(jnp.concatenate(p_rows, axis=0), v2, preferred_element_type=F32)
            for rb in range(2):
                o_ref[0, rows, cols[rb]] = o2[rb * w:(rb + 1) * w] * scales[rb]


def _swa_attention(proj, sinks):
    B, S, _ = proj.shape
    t = SWA_TILE
    w = WINDOW
    q_blk = (3 * SB_HEADS * HEAD_DIM) // (SWA_HEADS * HEAD_DIM)
    k_blk = (3 * SB_HEADS + SWA_HEADS) * HEAD_DIM // LANES
    v_blk = k_blk + 1
    prev = lambda n: jnp.maximum(n * (t // w) - 1, 0)
    bias = _swa_bias()
    return pl.pallas_call(
        _swa_kernel,
        grid=(B, S // t),
        in_specs=[
            pl.BlockSpec(memory_space=pltpu.SMEM),
            pl.BlockSpec(bias.shape, lambda b, n: (0, 0, 0, 0)),
            pl.BlockSpec((1, t, SWA_HEADS * HEAD_DIM), lambda b, n: (b, n, q_blk)),
            pl.BlockSpec((1, w, LANES), lambda b, n: (b, prev(n), k_blk)),
            pl.BlockSpec((1, t, LANES), lambda b, n: (b, n, k_blk)),
            pl.BlockSpec((1, w, LANES), lambda b, n: (b, prev(n), v_blk)),
            pl.BlockSpec((1, t, LANES), lambda b, n: (b, n, v_blk)),
        ],
        out_specs=pl.BlockSpec((1, t, SWA_HEADS * HEAD_DIM), lambda b, n: (b, n, 0)),
        out_shape=jax.ShapeDtypeStruct((B, S, SWA_HEADS * HEAD_DIM), F32),
        compiler_params=pltpu.CompilerParams(
            dimension_semantics=("parallel", "arbitrary"), vmem_limit_bytes=VMEM_LIMIT),
        name="swa_attn",
    )(sinks, bias, proj, proj, proj, proj, proj)


def _ff_chunks(d_ff):
    n_tiles = d_ff // MXU_TILE
    per = FF_CHUNK // MXU_TILE
    sizes = [per] * (n_tiles // per) + ([n_tiles % per] if n_tiles % per else [])
    return tuple(s * MXU_TILE for s in sizes)


def _post_kernel(ysb_ref, ysw_ref, x_ref, mod_ref, wo_ref, gsb_ref, gsw_ref, l1g_ref, l1b_ref,
                 wgu_ref, wd_ref, l2g_ref, l2b_ref, o_ref, *, alpha, chunks):
    sb_w = ysb_ref.shape[-1]
    d_ff = wd_ref.shape[0]
    mod = lambda k: mod_ref[0, k:k + 1, :]
    n_sb = _rms_norm(ysb_ref[0], gsb_ref[...]).astype(BF16)
    n_sw = _rms_norm(ysw_ref[0], gsw_ref[...]).astype(BF16)
    attn = (jnp.dot(n_sb, wo_ref[:sb_w, :], preferred_element_type=F32)
            + jnp.dot(n_sw, wo_ref[sb_w:, :], preferred_element_type=F32))
    x1 = _layer_norm(alpha * x_ref[0] + (1.0 + mod(2)) * attn, l1g_ref[...], l1b_ref[...])

    h = (x1 * (1.0 + mod(4)) + mod(3)).astype(BF16)
    ffn = None
    start = 0
    for width in chunks:
        gate = jnp.dot(h, wgu_ref[:, start:start + width], preferred_element_type=F32)
        up = jnp.dot(h, wgu_ref[:, d_ff + start:d_ff + start + width],
                     preferred_element_type=F32)
        a = (gate * jax.nn.sigmoid(gate) * up).astype(BF16)
        part = jnp.dot(a, wd_ref[start:start + width, :], preferred_element_type=F32)
        ffn = part if ffn is None else ffn + part
        start += width
    o_ref[0] = _layer_norm(alpha * x1 + (1.0 + mod(5)) * ffn, l2g_ref[...], l2b_ref[...])


def _post_attention(y_sb, y_sw, x, mod, w_out, gn_sb, gn_swa, ln1_g, ln1_b, w_gu, w_down,
                    ln2_g, ln2_b, alpha):
    B, S, D = x.shape
    W = y_sb.shape[-1]
    d_ff = w_down.shape[0]
    row = lambda b, i: (b, i, 0)
    const2 = lambda b, i: (0, 0)
    resident = lambda a: pl.BlockSpec(a.shape, const2, pipeline_mode=pl.Buffered(1))
    return pl.pallas_call(
        functools.partial(_post_kernel, alpha=alpha, chunks=_ff_chunks(d_ff)),
        grid=(B, S // ROW_TILE),
        in_specs=[
            pl.BlockSpec((1, ROW_TILE, W), row),
            pl.BlockSpec((1, ROW_TILE, W), row),
            pl.BlockSpec((1, ROW_TILE, D), row),
            pl.BlockSpec((1, N_MOD, D), lambda b, i: (b, 0, 0)),
            resident(w_out), resident(gn_sb), resident(gn_swa), resident(ln1_g), resident(ln1_b),
            resident(w_gu), resident(w_down), resident(ln2_g), resident(ln2_b),
        ],
        out_specs=pl.BlockSpec((1, ROW_TILE, D), row),
        out_shape=jax.ShapeDtypeStruct((B, S, D), F32),
        compiler_params=pltpu.CompilerParams(
            dimension_semantics=("parallel", "parallel"), vmem_limit_bytes=VMEM_LIMIT),
        name="post_attn",
    )(y_sb, y_sw, x, mod, w_out, gn_sb, gn_swa, ln1_g, ln1_b, w_gu, w_down, ln2_g, ln2_b)


def kernel(x, c, w_ada, b_ada, w_in, b_in, sinks, gn_sb, gn_swa, w_out, ln1_g, ln1_b,
           w_gu, w_down, ln2_g, ln2_b):
    depth = w_ada.shape[0]
    B, S, D = x.shape
    alpha = (2.0 * depth) ** 0.25
    sb_w = SB_HEADS * HEAD_DIM
    swa_qw = SWA_HEADS * HEAD_DIM
    d_in = w_in.shape[-1]
    cols = jnp.arange(d_in)
    qk_scale = 1.0 / math.sqrt(HEAD_DIM)
    col_scale = jnp.where(cols < sb_w, qk_scale * LOG2_E, 1.0)
    is_swa_q = (cols >= 3 * sb_w) & (cols < 3 * sb_w + swa_qw)
    col_scale = jnp.where(is_swa_q, qk_scale, col_scale).astype(F32)[None, :]
    tri = (jnp.arange(SB_TILE)[:, None] >= jnp.arange(SB_TILE)[None, :]).astype(BF16)
    row2 = lambda a: a.reshape(1, -1)

    for l in range(depth):
        mod = _mod(c, w_ada[l], row2(b_ada[l])).reshape(B, N_MOD, D)
        proj = _in_proj(x, mod, w_in[l].astype(BF16), row2(b_in[l]), col_scale)
        y_sb = _sb_attention(proj, tri)
        y_sw = _swa_attention(proj, sinks[l])
        x = _post_attention(y_sb, y_sw, x, mod, w_out[l].astype(BF16), row2(gn_sb[l]),
                            row2(gn_swa[l]), row2(ln1_g[l]), row2(ln1_b[l]),
                            w_gu[l].astype(BF16), w_down[l].astype(BF16), row2(ln2_g[l]),
                            row2(ln2_b[l]), alpha)
    return x
```

```python
import functools
import math

import jax
import jax.numpy as jnp
from jax import lax
from jax.experimental import pallas as pl
from jax.experimental.pallas import tpu as pltpu

F32 = jnp.float32
BF16 = jnp.bfloat16

HEAD_DIM = 64
SB_HEADS = 8
SWA_HEADS = 8
SWA_KV_HEADS = 2
SWA_GROUP = SWA_HEADS // SWA_KV_HEADS
WINDOW = 128
N_MOD = 6
LN_EPS = 1e-5
RMS_EPS = 1e-6
MASK_VALUE = -1e30
EXP2_UNDERFLOW = -150.0
LOG2_E = 1.4426950408889634
LANES = 128

SB_TILE = 256
SB_UNROLL = 2
SWA_TILE = 512
ROW_TILE = 512
POST_SPLIT = 2
MXU_TILE = 256
FF_CHUNK = 768
MOD_COLS = 1536
VMEM_LIMIT = 56 * 1024 * 1024


def _layer_norm(r, g, b):
    mu = jnp.mean(r, axis=-1, keepdims=True)
    d = r - mu
    var = jnp.mean(d * d, axis=-1, keepdims=True)
    return d * lax.rsqrt(var + LN_EPS) * g + b


def _rms_norm(y, g):
    ms = jnp.mean(y * y, axis=-1, keepdims=True)
    return y * lax.rsqrt(ms + RMS_EPS) * g


def _mod_kernel(ct_ref, w_ref, b_ref, o_ref):
    ct = ct_ref[...]
    s = ct * jax.nn.sigmoid(ct)
    w = w_ref[...]
    for b in range(ct.shape[1]):
        o_ref[b:b + 1, :] = jnp.sum(w * s[:, b:b + 1], axis=0, keepdims=True) + b_ref[...]


def _mod(c, w_ada, b_ada):
    B, D = c.shape
    N = w_ada.shape[1]
    return pl.pallas_call(
        _mod_kernel,
        grid=(N // MOD_COLS,),
        in_specs=[
            pl.BlockSpec((D, B), lambda n: (0, 0)),
            pl.BlockSpec((D, MOD_COLS), lambda n: (0, n)),
            pl.BlockSpec((1, MOD_COLS), lambda n: (0, n)),
        ],
        out_specs=pl.BlockSpec((B, MOD_COLS), lambda n: (0, n)),
        out_shape=jax.ShapeDtypeStruct((B, N), F32),
        compiler_params=pltpu.CompilerParams(
            dimension_semantics=("arbitrary",), vmem_limit_bytes=VMEM_LIMIT),
        name="mod",
    )(c.T, w_ada, b_ada)


def _inproj_kernel(x_ref, mod_ref, w_ref, b_ref, cs_ref, o_ref, wb_ref, *, col_chunk):
    @pl.when(jnp.logical_and(pl.program_id(0) == 0, pl.program_id(1) == 0))
    def _():
        wb_ref[...] = w_ref[...].astype(BF16)

    sh = mod_ref[0, 0:1, :]
    sc = mod_ref[0, 1:2, :]
    h = (x_ref[0] * (1.0 + sc) + sh).astype(BF16)
    n_cols = w_ref.shape[1]
    for n in range(n_cols // col_chunk):
        sl = slice(n * col_chunk, (n + 1) * col_chunk)
        p = jnp.dot(h, wb_ref[:, sl], preferred_element_type=F32)
        o_ref[0, :, sl] = ((p + b_ref[:, sl]) * cs_ref[:, sl]).astype(BF16)


def _in_proj(x, mod, w_in, b_in, col_scale):
    B, S, D = x.shape
    N = w_in.shape[1]
    return pl.pallas_call(
        functools.partial(_inproj_kernel, col_chunk=768),
        grid=(B, S // ROW_TILE),
        in_specs=[
            pl.BlockSpec((1, ROW_TILE, D), lambda b, i: (b, i, 0)),
            pl.BlockSpec((1, N_MOD, D), lambda b, i: (b, 0, 0)),
            pl.BlockSpec((D, N), lambda b, i: (0, 0), pipeline_mode=pl.Buffered(1)),
            pl.BlockSpec((1, N), lambda b, i: (0, 0)),
            pl.BlockSpec((1, N), lambda b, i: (0, 0)),
        ],
        out_specs=pl.BlockSpec((1, ROW_TILE, N), lambda b, i: (b, i, 0)),
        out_shape=jax.ShapeDtypeStruct((B, S, N), BF16),
        scratch_shapes=[pltpu.VMEM((D, N), BF16)],
        compiler_params=pltpu.CompilerParams(
            dimension_semantics=("arbitrary", "arbitrary"), vmem_limit_bytes=VMEM_LIMIT),
        name="in_proj",
    )(x, mod, w_in, b_in, col_scale)


def _sb_kernel(q_ref, k_ref, v_ref, u_ref, *rest):
    t = SB_TILE
    n_cast = (len(rest) - 1) // 2
    o_ref = rest[n_cast]
    n_tiles = q_ref.shape[1] // t
    u = u_ref[...]
    first_head = lambda rows: lax.broadcasted_iota(jnp.int32, (rows, LANES), 1) < HEAD_DIM
    row = lax.broadcasted_iota(jnp.int32, (2 * t, t), 0)
    col = lax.broadcasted_iota(jnp.int32, (2 * t, t), 1)
    before = col < (row & (t - 1))
    nt_dims = (((1,), (1,)), ((), ()))

    def log_one_minus_sigmoid(z):
        neg_part = jnp.minimum(z, 0.0)
        neg_relu = neg_part - z
        return neg_relu - jnp.log2(1.0 + jnp.exp2(neg_part + neg_relu))

    def suffix(lr):
        return jnp.dot(lr.astype(BF16), u, preferred_element_type=F32)

    def row_total(lr):
        return jnp.sum(lr, axis=-1, keepdims=True)

    def weighted_values(w, vals):
        w = w.astype(BF16)
        zero = jnp.zeros_like(vals)
        in_first = first_head(vals.shape[0])
        w2 = jnp.concatenate([w[:t], w[t:]], axis=1)
        v2 = jnp.concatenate([jnp.where(in_first, vals, zero), jnp.where(in_first, zero, vals)],
                             axis=0)
        return jnp.dot(w2, v2, preferred_element_type=F32)

    def score_diagonal_and_previous(i):
        here = pl.multiple_of(i * t, t)
        prev = pl.multiple_of(jnp.maximum(i - 1, 0) * t, t)
        q = q_ref[0, pl.ds(here, t), :]
        zero = jnp.zeros_like(q)
        in_first = first_head(t)
        qs = jnp.concatenate([jnp.where(in_first, q, zero), jnp.where(in_first, zero, q)], axis=0)

        keys = jnp.concatenate([k_ref[0, pl.ds(prev, t), :], k_ref[0, pl.ds(here, t), :]], axis=0)
        vals = jnp.concatenate([v_ref[0, pl.ds(prev, t), :], v_ref[0, pl.ds(here, t), :]], axis=0)
        z = lax.dot_general(qs, keys, nt_dims, preferred_element_type=F32)
        z = jnp.concatenate([z[:, :t], jnp.where(before, z[:, t:], MASK_VALUE)], axis=1)
        return qs, vals, z, log_one_minus_sigmoid(z)

    def weigh_diagonal_and_previous(i, qs, vals, z, lr):
        lr_p, lr_d = lr[:, :t], lr[:, t:]
        run_p = row_total(lr_d) + jnp.where(i == 0, MASK_VALUE, 0.0)
        w_d = jnp.exp2(z[:, t:] + suffix(lr_d))
        w_p = jnp.exp2(z[:, :t] + suffix(lr_p) + run_p)
        acc = weighted_values(jnp.concatenate([w_p, w_d], axis=1), vals)
        return qs, run_p + row_total(lr_p), acc

    def earlier_tiles(i, qs, run, acc):
        def cond(c):
            j, live, _, _ = c
            return jnp.logical_and(j >= 0, live)

        def body(c):
            j, _, run, acc = c
            start = pl.multiple_of(j * t, t)
            z = lax.dot_general(qs, k_ref[0, pl.ds(start, t), :], nt_dims,
                                preferred_element_type=F32)
            lr = log_one_minus_sigmoid(z)
            w = jnp.exp2(z + suffix(lr) + run)
            acc = acc + weighted_values(w, v_ref[0, pl.ds(start, t), :])
            run = run + row_total(lr)
            return j - 1, jnp.max(run) >= EXP2_UNDERFLOW, run, acc

        _, _, _, acc = lax.while_loop(
            cond, body, (i - 2, jnp.max(run) >= EXP2_UNDERFLOW, run, acc))
        o_ref[0, pl.ds(pl.multiple_of(i * t, t), t), :] = acc

    def q_tiles(g, carry):
        tiles = [g * SB_UNROLL + s for s in range(SB_UNROLL)]
        scored = [score_diagonal_and_previous(i) for i in tiles]
        heads = [weigh_diagonal_and_previous(i, *s) for i, s in zip(tiles, scored)]
        for i, (qs, run, acc) in zip(tiles, heads):
            earlier_tiles(i, qs, run, acc)
        return carry

    lax.fori_loop(0, n_tiles // SB_UNROLL, q_tiles, 0)

    for src_ref, dst_ref in zip(rest[:n_cast], rest[n_cast + 1:]):
        dst_ref[...] = src_ref[...].astype(BF16)


def _sb_attention(proj, u, weights):
    B, S, _ = proj.shape
    n_pairs = SB_HEADS * HEAD_DIM // LANES
    n_steps = B * n_pairs
    seq = lambda blk: pl.BlockSpec((1, S, LANES), lambda b, p: (b, 0, blk * n_pairs + p))

    def slice_spec(w, axis):
        block = list(w.shape)
        block[axis] = w.shape[axis] // n_steps
        assert block[axis] * n_steps == w.shape[axis] and block[axis] % (LANES if axis else 16) == 0
        step = lambda b, p: b * n_pairs + p
        index_map = (lambda b, p: (0, step(b, p))) if axis else (lambda b, p: (step(b, p), 0))
        return pl.BlockSpec(tuple(block), index_map)

    cast_specs = [slice_spec(w, axis) for w, axis in weights]
    outs = pl.pallas_call(
        _sb_kernel,
        grid=(B, n_pairs),
        in_specs=[seq(0), seq(1), seq(2), pl.BlockSpec(u.shape, lambda b, p: (0, 0))] + cast_specs,
        out_specs=[pl.BlockSpec((1, S, LANES), lambda b, p: (b, 0, p))] + cast_specs,
        out_shape=[jax.ShapeDtypeStruct((B, S, SB_HEADS * HEAD_DIM), F32)]
        + [jax.ShapeDtypeStruct(w.shape, BF16) for w, _ in weights],
        compiler_params=pltpu.CompilerParams(
            dimension_semantics=("parallel", "parallel"), vmem_limit_bytes=VMEM_LIMIT),
        name="sb_attn",
    )(proj, proj, proj, u, *[w for w, _ in weights])
    return outs[0], outs[1:]


def _swa_bias():
    w = WINDOW
    r = jnp.arange(2 * w)[:, None]
    c = jnp.arange(4 * w)[None, :]
    dist = (r % w) + w - (c % (2 * w))
    in_band = (dist >= 0) & (dist < w)
    groups = []
    for g in range(SWA_KV_HEADS):
        head = SWA_GROUP * g + 2 * (r // w) + (c // (2 * w))
        slope = jnp.exp2(-8.0 * (head + 1).astype(F32) / SWA_HEADS)
        groups.append(jnp.where(in_band, -slope * dist.astype(F32), MASK_VALUE))
    rest = jnp.stack(groups)
    first = jnp.where((c % (2 * w)) >= w, rest, MASK_VALUE)
    return jnp.stack([first, rest])


def _swa_kernel(sink_ref, bias_ref, q_ref, kp_ref, kc_ref, vp_ref, vc_ref, o_ref):
    w = WINDOW
    n = pl.program_id(1)
    lane = lax.broadcasted_iota(jnp.int32, (w, LANES), 1)
    lo = lane < HEAD_DIM
    nt_dims = (((1,), (1,)), ((), ()))

    def placed(prev_ref, cur_ref):
        band = jnp.concatenate([prev_ref[0], cur_ref[0]], axis=0).astype(F32)
        swapped = pltpu.roll(band, HEAD_DIM, axis=1)
        in_lo = lax.broadcasted_iota(jnp.int32, band.shape, 1) < HEAD_DIM
        return [
            (jnp.where(in_lo, band, 0.0).astype(BF16), jnp.where(in_lo, 0.0, swapped).astype(BF16)),
            (jnp.where(in_lo, swapped, 0.0).astype(BF16), jnp.where(in_lo, 0.0, band).astype(BF16)),
        ]

    k_placed = placed(kp_ref, kc_ref)
    v_placed = placed(vp_ref, vc_ref)
    band_lo = lax.broadcasted_iota(jnp.int32, (2 * w, LANES), 1) < HEAD_DIM
    ones_lo = jnp.where(band_lo, 1.0, 0.0).astype(BF16)
    ones_hi = jnp.where(band_lo, 0.0, 1.0).astype(BF16)
    variant = jnp.minimum(n, 1)
    for a in range(SWA_TILE // w):
        rows = slice(a * w, (a + 1) * w)
        band = slice(a * w, (a + 2) * w)
        for g in range(SWA_KV_HEADS):
            cols = [slice((2 * g + rb) * LANES, (2 * g + rb + 1) * LANES) for rb in range(2)]
            q2 = jnp.concatenate([q_ref[0, rows, cols[0]], q_ref[0, rows, cols[1]]], axis=0)
            k2 = jnp.concatenate([k_placed[g][0][band], k_placed[g][1][band]], axis=0)
            v2 = jnp.concatenate([jnp.concatenate([v_placed[g][0][band], ones_lo], axis=1),
                                  jnp.concatenate([v_placed[g][1][band], ones_hi], axis=1)], axis=0)
            bias = bias_ref[variant, g] if a == 0 else bias_ref[1, g]
            s = lax.dot_general(q2, k2, nt_dims, preferred_element_type=F32) + bias
            p_rows, sink_terms = [], []
            for rb in range(2):
                p_cols, sink_cols = [], []
                for half in range(2):
                    sink = sink_ref[SWA_GROUP * g + 2 * rb + half]
                    sq = s[rb * w:(rb + 1) * w, half * 2 * w:(half + 1) * 2 * w]
                    m = jnp.maximum(jnp.max(sq, axis=-1, keepdims=True), sink)
                    p_cols.append(jnp.exp(sq - m).astype(BF16))
                    sink_cols.append(jnp.exp(sink - m))
                p_rows.append(jnp.concatenate(p_cols, axis=1))
                sink_terms.append(jnp.where(lo, sink_cols[0], sink_cols[1]))
            o2 = jnp.dot(jnp.concatenate(p_rows, axis=0), v2, preferred_element_type=F32)
            for rb in range(2):
                sub = slice(rb * w, (rb + 1) * w)
                o_ref[0, rows, cols[rb]] = o2[sub, :LANES] / (o2[sub, LANES:] + sink_terms[rb])


def _swa_attention(proj, sinks):
    B, S, _ = proj.shape
    t = SWA_TILE
    w = WINDOW
    q_blk = (3 * SB_HEADS * HEAD_DIM) // (SWA_HEADS * HEAD_DIM)
    k_blk = (3 * SB_HEADS + SWA_HEADS) * HEAD_DIM // LANES
    v_blk = k_blk + 1
    prev = lambda n: jnp.maximum(n * (t // w) - 1, 0)
    bias = _swa_bias()
    return pl.pallas_call(
        _swa_kernel,
        grid=(B, S // t),
        in_specs=[
            pl.BlockSpec(memory_space=pltpu.SMEM),
            pl.BlockSpec(bias.shape, lambda b, n: (0, 0, 0, 0)),
            pl.BlockSpec((1, t, SWA_HEADS * HEAD_DIM), lambda b, n: (b, n, q_blk)),
            pl.BlockSpec((1, w, LANES), lambda b, n: (b, prev(n), k_blk)),
            pl.BlockSpec((1, t, LANES), lambda b, n: (b, n, k_blk)),
            pl.BlockSpec((1, w, LANES), lambda b, n: (b, prev(n), v_blk)),
            pl.BlockSpec((1, t, LANES), lambda b, n: (b, n, v_blk)),
        ],
        out_specs=pl.BlockSpec((1, t, SWA_HEADS * HEAD_DIM), lambda b, n: (b, n, 0)),
        out_shape=jax.ShapeDtypeStruct((B, S, SWA_HEADS * HEAD_DIM), F32),
        compiler_params=pltpu.CompilerParams(
            dimension_semantics=("parallel", "arbitrary"), vmem_limit_bytes=VMEM_LIMIT),
        name="swa_attn",
    )(sinks, bias, proj, proj, proj, proj, proj)


def _ff_chunks(d_ff):
    n_tiles = d_ff // MXU_TILE
    per = FF_CHUNK // MXU_TILE
    sizes = [per] * (n_tiles // per) + ([n_tiles % per] if n_tiles % per else [])
    return tuple(s * MXU_TILE for s in sizes)


def _post_kernel(ysb_ref, ysw_ref, x_ref, mod_ref, wo_ref, gsb_ref, gsw_ref, l1g_ref, l1b_ref,
                 wgu_ref, wd_ref, l2g_ref, l2b_ref, o_ref, *, alpha, chunks):
    sb_w = ysb_ref.shape[-1]
    d_ff = wd_ref.shape[0]
    mod = lambda k: mod_ref[0, k:k + 1, :]

    def attention_tail(rows):
        n_sb = _rms_norm(ysb_ref[0, rows, :], gsb_ref[...]).astype(BF16)
        n_sw = _rms_norm(ysw_ref[0, rows, :], gsw_ref[...]).astype(BF16)
        attn = (jnp.dot(n_sb, wo_ref[:sb_w, :], preferred_element_type=F32)
                + jnp.dot(n_sw, wo_ref[sb_w:, :], preferred_element_type=F32))
        return _layer_norm(alpha * x_ref[0, rows, :] + (1.0 + mod(2)) * attn,
                           l1g_ref[...], l1b_ref[...])

    def swiglu(x1):
        h = (x1 * (1.0 + mod(4)) + mod(3)).astype(BF16)
        ffn = None
        start = 0
        for width in chunks:
            gate = jnp.dot(h, wgu_ref[:, start:start + width], preferred_element_type=F32)
            up = jnp.dot(h, wgu_ref[:, d_ff + start:d_ff + start + width],
                         preferred_element_type=F32)
            a = (gate * jax.nn.sigmoid(gate) * up).astype(BF16)
            part = jnp.dot(a, wd_ref[start:start + width, :], preferred_element_type=F32)
            ffn = part if ffn is None else ffn + part
            start += width
        return ffn

    n_rows = x_ref.shape[1]
    groups = [slice(r, r + n_rows // POST_SPLIT) for r in range(0, n_rows, n_rows // POST_SPLIT)]
    x1s = [attention_tail(rows) for rows in groups]
    ffns = [swiglu(x1) for x1 in x1s]
    for rows, x1, ffn in zip(groups, x1s, ffns):
        o_ref[0, rows, :] = _layer_norm(alpha * x1 + (1.0 + mod(5)) * ffn,
                                        l2g_ref[...], l2b_ref[...])


def _post_attention(y_sb, y_sw, x, mod, w_out, gn_sb, gn_swa, ln1_g, ln1_b, w_gu, w_down,
                    ln2_g, ln2_b, alpha):
    B, S, D = x.shape
    W = y_sb.shape[-1]
    d_ff = w_down.shape[0]
    row = lambda b, i: (b, i, 0)
    const2 = lambda b, i: (0, 0)
    resident = lambda a: pl.BlockSpec(a.shape, const2, pipeline_mode=pl.Buffered(1))
    return pl.pallas_call(
        functools.partial(_post_kernel, alpha=alpha, chunks=_ff_chunks(d_ff)),
        grid=(B, S // ROW_TILE),
        in_specs=[
            pl.BlockSpec((1, ROW_TILE, W), row),
            pl.BlockSpec((1, ROW_TILE, W), row),
            pl.BlockSpec((1, ROW_TILE, D), row),
            pl.BlockSpec((1, N_MOD, D), lambda b, i: (b, 0, 0)),
            resident(w_out), resident(gn_sb), resident(gn_swa), resident(ln1_g), resident(ln1_b),
            resident(w_gu), resident(w_down), resident(ln2_g), resident(ln2_b),
        ],
        out_specs=pl.BlockSpec((1, ROW_TILE, D), row),
        out_shape=jax.ShapeDtypeStruct((B, S, D), F32),
        compiler_params=pltpu.CompilerParams(
            dimension_semantics=("parallel", "parallel"), vmem_limit_bytes=VMEM_LIMIT),
        name="post_attn",
    )(y_sb, y_sw, x, mod, w_out, gn_sb, gn_swa, ln1_g, ln1_b, w_gu, w_down, ln2_g, ln2_b)


def kernel(x, c, w_ada, b_ada, w_in, b_in, sinks, gn_sb, gn_swa, w_out, ln1_g, ln1_b,
           w_gu, w_down, ln2_g, ln2_b):
    depth = w_ada.shape[0]
    B, S, D = x.shape
    alpha = (2.0 * depth) ** 0.25
    sb_w = SB_HEADS * HEAD_DIM
    swa_qw = SWA_HEADS * HEAD_DIM
    d_in = w_in.shape[-1]
    cols = jnp.arange(d_in)
    qk_scale = 1.0 / math.sqrt(HEAD_DIM)
    col_scale = jnp.where(cols < sb_w, qk_scale * LOG2_E, 1.0)
    is_swa_q = (cols >= 3 * sb_w) & (cols < 3 * sb_w + swa_qw)
    col_scale = jnp.where(is_swa_q, qk_scale, col_scale).astype(F32)[None, :]
    tri = (jnp.arange(SB_TILE)[:, None] >= jnp.arange(SB_TILE)[None, :]).astype(BF16)
    row2 = lambda a: a.reshape(1, -1)

    for l in range(depth):
        mod = _mod(c, w_ada[l], row2(b_ada[l])).reshape(B, N_MOD, D)
        proj = _in_proj(x, mod, w_in[l], row2(b_in[l]), col_scale)
        y_sb, (w_o, w_g, w_d) = _sb_attention(
            proj, tri, [(w_out[l], 0), (w_gu[l], 0), (w_down[l], 0)])
        y_sw = _swa_attention(proj, sinks[l])
        x = _post_attention(y_sb, y_sw, x, mod, w_o, row2(gn_sb[l]), row2(gn_swa[l]),
                            row2(ln1_g[l]), row2(ln1_b[l]), w_g, w_d, row2(ln2_g[l]),
                            row2(ln2_b[l]), alpha)
    return x
```

```python
import functools
import math

import jax
import jax.numpy as jnp
from jax import lax
from jax.experimental import pallas as pl
from jax.experimental.pallas import tpu as pltpu

F32 = jnp.float32
BF16 = jnp.bfloat16

HEAD_DIM = 64
SB_HEADS = 8
SWA_HEADS = 8
SWA_KV_HEADS = 2
SWA_GROUP = SWA_HEADS // SWA_KV_HEADS
WINDOW = 128
N_MOD = 6
LN_EPS = 1e-5
RMS_EPS = 1e-6
MASK_VALUE = -1e30
EXP2_UNDERFLOW = -150.0
LOG2_E = 1.4426950408889634
LANES = 128

SB_TILE = 256
SB_UNROLL = 2
SWA_TILE = 1024
ROW_TILE = 1024
POST_ROW_TILE = 1024
POST_SPLIT = 4
MXU_TILE = 256
FF_CHUNK = 768
MOD_COLS = 1536
VMEM_LIMIT = 60 * 1024 * 1024


def _layer_norm(r, g, b):
    mu = jnp.mean(r, axis=-1, keepdims=True)
    d = r - mu
    var = jnp.mean(d * d, axis=-1, keepdims=True)
    return d * lax.rsqrt(var + LN_EPS) * g + b


def _rms_norm(y, g):
    ms = jnp.mean(y * y, axis=-1, keepdims=True)
    return y * lax.rsqrt(ms + RMS_EPS) * g


def _mod_kernel(ct_ref, w_ref, b_ref, o_ref):
    ct = ct_ref[...]
    s = ct * jax.nn.sigmoid(ct)
    w = w_ref[...]
    for b in range(ct.shape[1]):
        o_ref[b:b + 1, :] = jnp.sum(w * s[:, b:b + 1], axis=0, keepdims=True) + b_ref[...]


def _mod(c, w_ada, b_ada):
    B, D = c.shape
    N = w_ada.shape[1]
    return pl.pallas_call(
        _mod_kernel,
        grid=(N // MOD_COLS,),
        in_specs=[
            pl.BlockSpec((D, B), lambda n: (0, 0)),
            pl.BlockSpec((D, MOD_COLS), lambda n: (0, n)),
            pl.BlockSpec((1, MOD_COLS), lambda n: (0, n)),
        ],
        out_specs=pl.BlockSpec((B, MOD_COLS), lambda n: (0, n)),
        out_shape=jax.ShapeDtypeStruct((B, N), F32),
        compiler_params=pltpu.CompilerParams(
            dimension_semantics=("arbitrary",), vmem_limit_bytes=VMEM_LIMIT),
        name="mod",
    )(c.T, w_ada, b_ada)


def _inproj_kernel(x_ref, mod_ref, w_ref, b_ref, cs_ref, o_ref, wb_ref, *, col_chunk):
    @pl.when(jnp.logical_and(pl.program_id(0) == 0, pl.program_id(1) == 0))
    def _():
        wb_ref[...] = w_ref[...].astype(BF16)

    sh = mod_ref[0, 0:1, :]
    sc = mod_ref[0, 1:2, :]
    h = (x_ref[0] * (1.0 + sc) + sh).astype(BF16)
    n_cols = w_ref.shape[1]
    for n in range(n_cols // col_chunk):
        sl = slice(n * col_chunk, (n + 1) * col_chunk)
        p = jnp.dot(h, wb_ref[:, sl], preferred_element_type=F32)
        o_ref[0, :, sl] = ((p + b_ref[:, sl]) * cs_ref[:, sl]).astype(BF16)


def _in_proj(x, mod, w_in, b_in, col_scale):
    B, S, D = x.shape
    N = w_in.shape[1]
    return pl.pallas_call(
        functools.partial(_inproj_kernel, col_chunk=768),
        grid=(B, S // ROW_TILE),
        in_specs=[
            pl.BlockSpec((1, ROW_TILE, D), lambda b, i: (b, i, 0)),
            pl.BlockSpec((1, N_MOD, D), lambda b, i: (b, 0, 0)),
            pl.BlockSpec((D, N), lambda b, i: (0, 0), pipeline_mode=pl.Buffered(1)),
            pl.BlockSpec((1, N), lambda b, i: (0, 0)),
            pl.BlockSpec((1, N), lambda b, i: (0, 0)),
        ],
        out_specs=pl.BlockSpec((1, ROW_TILE, N), lambda b, i: (b, i, 0)),
        out_shape=jax.ShapeDtypeStruct((B, S, N), BF16),
        scratch_shapes=[pltpu.VMEM((D, N), BF16)],
        compiler_params=pltpu.CompilerParams(
            dimension_semantics=("arbitrary", "arbitrary"), vmem_limit_bytes=VMEM_LIMIT),
        name="in_proj",
    )(x, mod, w_in, b_in, col_scale)


def _sb_kernel(q_ref, k_ref, v_ref, u_ref, *rest):
    t = SB_TILE
    n_cast = (len(rest) - 1) // 2
    o_ref = rest[n_cast]
    n_tiles = q_ref.shape[1] // t
    u = u_ref[...]
    first_head = lambda rows: lax.broadcasted_iota(jnp.int32, (rows, LANES), 1) < HEAD_DIM
    row = lax.broadcasted_iota(jnp.int32, (2 * t, t), 0)
    col = lax.broadcasted_iota(jnp.int32, (2 * t, t), 1)
    before = col < (row & (t - 1))
    nt_dims = (((1,), (1,)), ((), ()))

    def log_one_minus_sigmoid(z):
        neg_part = jnp.minimum(z, 0.0)
        neg_relu = neg_part - z
        return neg_relu - jnp.log2(1.0 + jnp.exp2(neg_part + neg_relu))

    def suffix(lr):
        return jnp.dot(lr.astype(BF16), u, preferred_element_type=F32)

    def row_total(lr):
        return jnp.sum(lr, axis=-1, keepdims=True)

    def weighted_values(w, vals):
        w = w.astype(BF16)
        zero = jnp.zeros_like(vals)
        in_first = first_head(vals.shape[0])
        w2 = jnp.concatenate([w[:t], w[t:]], axis=1)
        v2 = jnp.concatenate([jnp.where(in_first, vals, zero), jnp.where(in_first, zero, vals)],
                             axis=0)
        return jnp.dot(w2, v2, preferred_element_type=F32)

    def score_diagonal_and_previous(i):
        here = pl.multiple_of(i * t, t)
        prev = pl.multiple_of(jnp.maximum(i - 1, 0) * t, t)
        q = q_ref[0, pl.ds(here, t), :]
        zero = jnp.zeros_like(q)
        in_first = first_head(t)
        qs = jnp.concatenate([jnp.where(in_first, q, zero), jnp.where(in_first, zero, q)], axis=0)

        keys = jnp.concatenate([k_ref[0, pl.ds(prev, t), :], k_ref[0, pl.ds(here, t), :]], axis=0)
        vals = jnp.concatenate([v_ref[0, pl.ds(prev, t), :], v_ref[0, pl.ds(here, t), :]], axis=0)
        z = lax.dot_general(qs, keys, nt_dims, preferred_element_type=F32)
        z = jnp.concatenate([z[:, :t], jnp.where(before, z[:, t:], MASK_VALUE)], axis=1)
        return qs, vals, z, log_one_minus_sigmoid(z)

    def weigh_diagonal_and_previous(i, qs, vals, z, lr):
        lr_p, lr_d = lr[:, :t], lr[:, t:]
        run_p = row_total(lr_d) + jnp.where(i == 0, MASK_VALUE, 0.0)
        w_d = jnp.exp2(z[:, t:] + suffix(lr_d))
        w_p = jnp.exp2(z[:, :t] + suffix(lr_p) + run_p)
        acc = weighted_values(jnp.concatenate([w_p, w_d], axis=1), vals)
        return qs, run_p + row_total(lr_p), acc

    def earlier_tiles(i, qs, run, acc):
        def cond(c):
            j, live, _, _ = c
            return jnp.logical_and(j >= 0, live)

        def body(c):
            j, _, run, acc = c
            start = pl.multiple_of(j * t, t)
            z = lax.dot_general(qs, k_ref[0, pl.ds(start, t), :], nt_dims,
                                preferred_element_type=F32)
            lr = log_one_minus_sigmoid(z)
            w = jnp.exp2(z + suffix(lr) + run)
            acc = acc + weighted_values(w, v_ref[0, pl.ds(start, t), :])
            run = run + row_total(lr)
            return j - 1, jnp.max(run) >= EXP2_UNDERFLOW, run, acc

        _, _, _, acc = lax.while_loop(
            cond, body, (i - 2, jnp.max(run) >= EXP2_UNDERFLOW, run, acc))
        o_ref[0, pl.ds(pl.multiple_of(i * t, t), t), :] = acc

    def q_tiles(g, carry):
        tiles = [g * SB_UNROLL + s for s in range(SB_UNROLL)]
        scored = [score_diagonal_and_previous(i) for i in tiles]
        heads = [weigh_diagonal_and_previous(i, *s) for i, s in zip(tiles, scored)]
        for i, (qs, run, acc) in zip(tiles, heads):
            earlier_tiles(i, qs, run, acc)
        return carry

    lax.fori_loop(0, n_tiles // SB_UNROLL, q_tiles, 0)

    for src_ref, dst_ref in zip(rest[:n_cast], rest[n_cast + 1:]):
        dst_ref[...] = src_ref[...].astype(BF16)


def _sb_attention(proj, u, weights):
    B, S, _ = proj.shape
    n_pairs = SB_HEADS * HEAD_DIM // LANES
    n_steps = B * n_pairs
    seq = lambda blk: pl.BlockSpec((1, S, LANES), lambda b, p: (b, 0, blk * n_pairs + p))

    def slice_spec(w, axis):
        block = list(w.shape)
        block[axis] = w.shape[axis] // n_steps
        assert block[axis] * n_steps == w.shape[axis] and block[axis] % (LANES if axis else 16) == 0
        step = lambda b, p: b * n_pairs + p
        index_map = (lambda b, p: (0, step(b, p))) if axis else (lambda b, p: (step(b, p), 0))
        return pl.BlockSpec(tuple(block), index_map)

    cast_specs = [slice_spec(w, axis) for w, axis in weights]
    outs = pl.pallas_call(
        _sb_kernel,
        grid=(B, n_pairs),
        in_specs=[seq(0), seq(1), seq(2), pl.BlockSpec(u.shape, lambda b, p: (0, 0))] + cast_specs,
        out_specs=[pl.BlockSpec((1, S, LANES), lambda b, p: (b, 0, p))] + cast_specs,
        out_shape=[jax.ShapeDtypeStruct((B, S, SB_HEADS * HEAD_DIM), F32)]
        + [jax.ShapeDtypeStruct(w.shape, BF16) for w, _ in weights],
        compiler_params=pltpu.CompilerParams(
            dimension_semantics=("parallel", "parallel"), vmem_limit_bytes=VMEM_LIMIT),
        name="sb_attn",
    )(proj, proj, proj, u, *[w for w, _ in weights])
    return outs[0], outs[1:]


def _swa_bias():
    w = WINDOW
    r = jnp.arange(2 * w)[:, None]
    c = jnp.arange(4 * w)[None, :]
    dist = (r % w) + w - (c % (2 * w))
    in_band = (dist >= 0) & (dist < w)
    groups = []
    for g in range(SWA_KV_HEADS):
        head = SWA_GROUP * g + 2 * (r // w) + (c // (2 * w))
        slope = jnp.exp2(-8.0 * (head + 1).astype(F32) / SWA_HEADS)
        groups.append(jnp.where(in_band, -slope * dist.astype(F32), MASK_VALUE))
    rest = jnp.stack(groups)
    first = jnp.where((c % (2 * w)) >= w, rest, MASK_VALUE)
    return jnp.stack([first, rest])


def _swa_kernel(sink_ref, bias_ref, q_ref, kp_ref, kc_ref, vp_ref, vc_ref, o_ref):
    w = WINDOW
    n = pl.program_id(1)
    lane = lax.broadcasted_iota(jnp.int32, (w, LANES), 1)
    lo = lane < HEAD_DIM
    nt_dims = (((1,), (1,)), ((), ()))

    def placed(prev_ref, cur_ref):
        band = jnp.concatenate([prev_ref[0], cur_ref[0]], axis=0).astype(F32)
        swapped = pltpu.roll(band, HEAD_DIM, axis=1)
        in_lo = lax.broadcasted_iota(jnp.int32, band.shape, 1) < HEAD_DIM
        return [
            (jnp.where(in_lo, band, 0.0).astype(BF16), jnp.where(in_lo, 0.0, swapped).astype(BF16)),
            (jnp.where(in_lo, swapped, 0.0).astype(BF16), jnp.where(in_lo, 0.0, band).astype(BF16)),
        ]

    k_placed = placed(kp_ref, kc_ref)
    v_placed = placed(vp_ref, vc_ref)
    band_lo = lax.broadcasted_iota(jnp.int32, (2 * w, LANES), 1) < HEAD_DIM
    ones_lo = jnp.where(band_lo, 1.0, 0.0).astype(BF16)
    ones_hi = jnp.where(band_lo, 0.0, 1.0).astype(BF16)
    variant = jnp.minimum(n, 1)
    for a in range(SWA_TILE // w):
        rows = slice(a * w, (a + 1) * w)
        band = slice(a * w, (a + 2) * w)
        for g in range(SWA_KV_HEADS):
            cols = [slice((2 * g + rb) * LANES, (2 * g + rb + 1) * LANES) for rb in range(2)]
            q2 = jnp.concatenate([q_ref[0, rows, cols[0]], q_ref[0, rows, cols[1]]], axis=0)
            k2 = jnp.concatenate([k_placed[g][0][band], k_placed[g][1][band]], axis=0)
            v2 = jnp.concatenate([jnp.concatenate([v_placed[g][0][band], ones_lo], axis=1),
                                  jnp.concatenate([v_placed[g][1][band], ones_hi], axis=1)], axis=0)
            bias = bias_ref[variant, g] if a == 0 else bias_ref[1, g]
            s = lax.dot_general(q2, k2, nt_dims, preferred_element_type=F32) + bias
            p_rows, sink_terms = [], []
            for rb in range(2):
                p_cols, sink_cols = [], []
                for half in range(2):
                    sink = sink_ref[SWA_GROUP * g + 2 * rb + half]
                    sq = s[rb * w:(rb + 1) * w, half * 2 * w:(half + 1) * 2 * w]
                    m = jnp.maximum(jnp.max(sq, axis=-1, keepdims=True), sink)
                    p_cols.append(jnp.exp(sq - m).astype(BF16))
                    sink_cols.append(jnp.exp(sink - m))
                p_rows.append(jnp.concatenate(p_cols, axis=1))
                sink_terms.append(jnp.where(lo, sink_cols[0], sink_cols[1]))
            o2 = jnp.dot(jnp.concatenate(p_rows, axis=0), v2, preferred_element_type=F32)
            for rb in range(2):
                sub = slice(rb * w, (rb + 1) * w)
                o_ref[0, rows, cols[rb]] = o2[sub, :LANES] / (o2[sub, LANES:] + sink_terms[rb])


def _swa_attention(proj, sinks):
    B, S, _ = proj.shape
    t = SWA_TILE
    w = WINDOW
    q_blk = (3 * SB_HEADS * HEAD_DIM) // (SWA_HEADS * HEAD_DIM)
    k_blk = (3 * SB_HEADS + SWA_HEADS) * HEAD_DIM // LANES
    v_blk = k_blk + 1
    prev = lambda n: jnp.maximum(n * (t // w) - 1, 0)
    bias = _swa_bias()
    return pl.pallas_call(
        _swa_kernel,
        grid=(B, S // t),
        in_specs=[
            pl.BlockSpec(memory_space=pltpu.SMEM),
            pl.BlockSpec(bias.shape, lambda b, n: (0, 0, 0, 0)),
            pl.BlockSpec((1, t, SWA_HEADS * HEAD_DIM), lambda b, n: (b, n, q_blk)),
            pl.BlockSpec((1, w, LANES), lambda b, n: (b, prev(n), k_blk)),
            pl.BlockSpec((1, t, LANES), lambda b, n: (b, n, k_blk)),
            pl.BlockSpec((1, w, LANES), lambda b, n: (b, prev(n), v_blk)),
            pl.BlockSpec((1, t, LANES), lambda b, n: (b, n, v_blk)),
        ],
        out_specs=pl.BlockSpec((1, t, SWA_HEADS * HEAD_DIM), lambda b, n: (b, n, 0)),
        out_shape=jax.ShapeDtypeStruct((B, S, SWA_HEADS * HEAD_DIM), F32),
        compiler_params=pltpu.CompilerParams(
            dimension_semantics=("parallel", "arbitrary"), vmem_limit_bytes=VMEM_LIMIT),
        name="swa_attn",
    )(sinks, bias, proj, proj, proj, proj, proj)


def _ff_chunks(d_ff):
    n_tiles = d_ff // MXU_TILE
    per = FF_CHUNK // MXU_TILE
    sizes = [per] * (n_tiles // per) + ([n_tiles % per] if n_tiles % per else [])
    return tuple(s * MXU_TILE for s in sizes)


def _post_kernel(ysb_ref, ysw_ref, x_ref, mod_ref, wo_ref, gsb_ref, gsw_ref, l1g_ref, l1b_ref,
                 wgu_ref, wd_ref, l2g_ref, l2b_ref, o_ref, *, alpha, chunks):
    sb_w = ysb_ref.shape[-1]
    d_ff = wd_ref.shape[0]
    mod = lambda k: mod_ref[0, k:k + 1, :]

    def attention_tail(rows):
        n_sb = _rms_norm(ysb_ref[0, rows, :], gsb_ref[...]).astype(BF16)
        n_sw = _rms_norm(ysw_ref[0, rows, :], gsw_ref[...]).astype(BF16)
        attn = (jnp.dot(n_sb, wo_ref[:sb_w, :], preferred_element_type=F32)
                + jnp.dot(n_sw, wo_ref[sb_w:, :], preferred_element_type=F32))
        return _layer_norm(alpha * x_ref[0, rows, :] + (1.0 + mod(2)) * attn,
                           l1g_ref[...], l1b_ref[...])

    def swiglu(x1):
        h = (x1 * (1.0 + mod(4)) + mod(3)).astype(BF16)
        ffn = None
        start = 0
        for width in chunks:
            gate = jnp.dot(h, wgu_ref[:, start:start + width], preferred_element_type=F32)
            up = jnp.dot(h, wgu_ref[:, d_ff + start:d_ff + start + width],
                         preferred_element_type=F32)
            a = (gate * jax.nn.sigmoid(gate) * up).astype(BF16)
            part = jnp.dot(a, wd_ref[start:start + width, :], preferred_element_type=F32)
            ffn = part if ffn is None else ffn + part
            start += width
        return ffn

    n_rows = x_ref.shape[1]
    groups = [slice(r, r + n_rows // POST_SPLIT) for r in range(0, n_rows, n_rows // POST_SPLIT)]
    x1s = [attention_tail(rows) for rows in groups]
    ffns = [swiglu(x1) for x1 in x1s]
    for rows, x1, ffn in zip(groups, x1s, ffns):
        o_ref[0, rows, :] = _layer_norm(alpha * x1 + (1.0 + mod(5)) * ffn,
                                        l2g_ref[...], l2b_ref[...])


def _post_attention(y_sb, y_sw, x, mod, w_out, gn_sb, gn_swa, ln1_g, ln1_b, w_gu, w_down,
                    ln2_g, ln2_b, alpha):
    B, S, D = x.shape
    W = y_sb.shape[-1]
    d_ff = w_down.shape[0]
    row = lambda b, i: (b, i, 0)
    const2 = lambda b, i: (0, 0)
    resident = lambda a: pl.BlockSpec(a.shape, const2, pipeline_mode=pl.Buffered(1))
    return pl.pallas_call(
        functools.partial(_post_kernel, alpha=alpha, chunks=_ff_chunks(d_ff)),
        grid=(B, S // POST_ROW_TILE),
        in_specs=[
            pl.BlockSpec((1, POST_ROW_TILE, W), row),
            pl.BlockSpec((1, POST_ROW_TILE, W), row),
            pl.BlockSpec((1, POST_ROW_TILE, D), row),
            pl.BlockSpec((1, N_MOD, D), lambda b, i: (b, 0, 0)),
            resident(w_out), resident(gn_sb), resident(gn_swa), resident(ln1_g), resident(ln1_b),
            resident(w_gu), resident(w_down), resident(ln2_g), resident(ln2_b),
        ],
        out_specs=pl.BlockSpec((1, POST_ROW_TILE, D), row),
        out_shape=jax.ShapeDtypeStruct((B, S, D), F32),
        compiler_params=pltpu.CompilerParams(
            dimension_semantics=("parallel", "parallel"), vmem_limit_bytes=VMEM_LIMIT),
        name="post_attn",
    )(y_sb, y_sw, x, mod, w_out, gn_sb, gn_swa, ln1_g, ln1_b, w_gu, w_down, ln2_g, ln2_b)


def kernel(x, c, w_ada, b_ada, w_in, b_in, sinks, gn_sb, gn_swa, w_out, ln1_g, ln1_b,
           w_gu, w_down, ln2_g, ln2_b):
    depth = w_ada.shape[0]
    B, S, D = x.shape
    alpha = (2.0 * depth) ** 0.25
    sb_w = SB_HEADS * HEAD_DIM
    swa_qw = SWA_HEADS * HEAD_DIM
    d_in = w_in.shape[-1]
    cols = jnp.arange(d_in)
    qk_scale = 1.0 / math.sqrt(HEAD_DIM)
    col_scale = jnp.where(cols < sb_w, qk_scale * LOG2_E, 1.0)
    is_swa_q = (cols >= 3 * sb_w) & (cols < 3 * sb_w + swa_qw)
    col_scale = jnp.where(is_swa_q, qk_scale, col_scale).astype(F32)[None, :]
    tri = (jnp.arange(SB_TILE)[:, None] >= jnp.arange(SB_TILE)[None, :]).astype(BF16)
    row2 = lambda a: a.reshape(1, -1)

    for l in range(depth):
        mod = _mod(c, w_ada[l], row2(b_ada[l])).reshape(B, N_MOD, D)
        proj = _in_proj(x, mod, w_in[l], row2(b_in[l]), col_scale)
        y_sb, (w_o, w_g, w_d) = _sb_attention(
            proj, tri, [(w_out[l], 0), (w_gu[l], 0), (w_down[l], 0)])
        y_sw = _swa_attention(proj, sinks[l])
        x = _post_attention(y_sb, y_sw, x, mod, w_o, row2(gn_sb[l]), row2(gn_swa[l]),
                            row2(ln1_g[l]), row2(ln1_b[l]), w_g, w_d, row2(ln2_g[l]),
                            row2(ln2_b[l]), alpha)
    return x
```

```python
import functools
import math

import jax
import jax.numpy as jnp
import numpy as np
from jax import lax
from jax.experimental import pallas as pl
from jax.experimental.pallas import tpu as pltpu

F32 = jnp.float32
BF16 = jnp.bfloat16

HEAD_DIM = 64
SB_HEADS = 8
SWA_HEADS = 8
SWA_KV_HEADS = 2
SWA_GROUP = SWA_HEADS // SWA_KV_HEADS
WINDOW = 128
LN_EPS = 1e-5
RMS_EPS = 1e-6
MASK_VALUE = -1e30
EXP2_UNDERFLOW = -150.0
LOG2_E = 1.4426950408889634
LANES = 128

SB_TILE = 256
SB_UNROLL = 2
SWA_TILE = 1024
ROW_TILE = 1024
POST_ROW_TILE = 1024
POST_SPLIT = 4
MXU_TILE = 256
FF_CHUNK = 768
MOD_COLS = 1536
VMEM_LIMIT = 60 * 1024 * 1024


def _layer_norm(r, g, b):
    mu = jnp.mean(r, axis=-1, keepdims=True)
    d = r - mu
    var = jnp.mean(d * d, axis=-1, keepdims=True)
    return d * lax.rsqrt(var + LN_EPS) * g + b


def _rms_norm(y, g):
    ms = jnp.mean(y * y, axis=-1, keepdims=True)
    return y * lax.rsqrt(ms + RMS_EPS) * g


def _mod_kernel(ct_ref, w_ref, b_ref, o_ref):
    ct = ct_ref[...]
    s = ct * jax.nn.sigmoid(ct)
    w = w_ref[...]
    for b in range(ct.shape[1]):
        o_ref[b:b + 1, :] = jnp.sum(w * s[:, b:b + 1], axis=0, keepdims=True) + b_ref[...]


def _mod(c, w_ada, b_ada):
    B, D = c.shape
    N = w_ada.shape[1]
    return pl.pallas_call(
        _mod_kernel,
        grid=(N // MOD_COLS,),
        in_specs=[
            pl.BlockSpec((D, B), lambda n: (0, 0)),
            pl.BlockSpec((D, MOD_COLS), lambda n: (0, n)),
            pl.BlockSpec((1, MOD_COLS), lambda n: (0, n)),
        ],
        out_specs=pl.BlockSpec((B, MOD_COLS), lambda n: (0, n)),
        out_shape=jax.ShapeDtypeStruct((B, N), F32),
        compiler_params=pltpu.CompilerParams(
            dimension_semantics=("arbitrary",), vmem_limit_bytes=VMEM_LIMIT),
        name="mod",
    )(c.T, w_ada, b_ada)


def _inproj_kernel(x_ref, mod_ref, w_ref, b_ref, cs_ref, o_ref, wb_ref, *, col_chunk):
    @pl.when(jnp.logical_and(pl.program_id(0) == 0, pl.program_id(1) == 0))
    def _():
        wb_ref[...] = w_ref[...].astype(BF16)

    d = x_ref.shape[-1]
    batch_row = pl.ds(pl.program_id(0), 1)
    sh = mod_ref[batch_row, 0:d]
    sc = mod_ref[batch_row, d:2 * d]
    h = (x_ref[0] * (1.0 + sc) + sh).astype(BF16)
    n_cols = w_ref.shape[1]
    for n in range(n_cols // col_chunk):
        sl = slice(n * col_chunk, (n + 1) * col_chunk)
        p = jnp.dot(h, wb_ref[:, sl], preferred_element_type=F32)
        o_ref[0, :, sl] = ((p + b_ref[:, sl]) * cs_ref[:, sl]).astype(BF16)


def _in_proj(x, mod, w_in, b_in, col_scale):
    B, S, D = x.shape
    N = w_in.shape[1]
    return pl.pallas_call(
        functools.partial(_inproj_kernel, col_chunk=768),
        grid=(B, S // ROW_TILE),
        in_specs=[
            pl.BlockSpec((1, ROW_TILE, D), lambda b, i: (b, i, 0)),
            pl.BlockSpec(mod.shape, lambda b, i: (0, 0)),
            pl.BlockSpec((D, N), lambda b, i: (0, 0), pipeline_mode=pl.Buffered(1)),
            pl.BlockSpec((1, N), lambda b, i: (0, 0)),
            pl.BlockSpec((1, N), lambda b, i: (0, 0)),
        ],
        out_specs=pl.BlockSpec((1, ROW_TILE, N), lambda b, i: (b, i, 0)),
        out_shape=jax.ShapeDtypeStruct((B, S, N), BF16),
        scratch_shapes=[pltpu.VMEM((D, N), BF16)],
        compiler_params=pltpu.CompilerParams(
            dimension_semantics=("arbitrary", "arbitrary"), vmem_limit_bytes=VMEM_LIMIT),
        name="in_proj",
    )(x, mod, w_in, b_in, col_scale)


def _sb_kernel(q_ref, k_ref, v_ref, u_ref, *rest):
    t = SB_TILE
    n_cast = (len(rest) - 1) // 2
    o_ref = rest[n_cast]
    n_tiles = q_ref.shape[1] // t
    u = u_ref[...]
    first_head = lambda rows: lax.broadcasted_iota(jnp.int32, (rows, LANES), 1) < HEAD_DIM
    row = lax.broadcasted_iota(jnp.int32, (2 * t, t), 0)
    col = lax.broadcasted_iota(jnp.int32, (2 * t, t), 1)
    before = col < (row & (t - 1))
    nt_dims = (((1,), (1,)), ((), ()))

    def log_one_minus_sigmoid(z):
        neg_part = jnp.minimum(z, 0.0)
        neg_relu = neg_part - z
        return neg_relu - jnp.log2(1.0 + jnp.exp2(neg_part + neg_relu))

    def suffix(lr):
        return jnp.dot(lr.astype(BF16), u, preferred_element_type=F32)

    def row_total(lr):
        return jnp.sum(lr, axis=-1, keepdims=True)

    def weighted_values(w, vals):
        w = w.astype(BF16)
        zero = jnp.zeros_like(vals)
        in_first = first_head(vals.shape[0])
        w2 = jnp.concatenate([w[:t], w[t:]], axis=1)
        v2 = jnp.concatenate([jnp.where(in_first, vals, zero), jnp.where(in_first, zero, vals)],
                             axis=0)
        return jnp.dot(w2, v2, preferred_element_type=F32)

    def score_diagonal_and_previous(i):
        here = pl.multiple_of(i * t, t)
        prev = pl.multiple_of(jnp.maximum(i - 1, 0) * t, t)
        q = q_ref[0, pl.ds(here, t), :]
        zero = jnp.zeros_like(q)
        in_first = first_head(t)
        qs = jnp.concatenate([jnp.where(in_first, q, zero), jnp.where(in_first, zero, q)], axis=0)

        keys = jnp.concatenate([k_ref[0, pl.ds(prev, t), :], k_ref[0, pl.ds(here, t), :]], axis=0)
        vals = jnp.concatenate([v_ref[0, pl.ds(prev, t), :], v_ref[0, pl.ds(here, t), :]], axis=0)
        z = lax.dot_general(qs, keys, nt_dims, preferred_element_type=F32)
        z = jnp.concatenate([z[:, :t], jnp.where(before, z[:, t:], MASK_VALUE)], axis=1)
        return qs, vals, z, log_one_minus_sigmoid(z)

    def weigh_diagonal_and_previous(i, qs, vals, z, lr):
        lr_p, lr_d = lr[:, :t], lr[:, t:]
        run_p = row_total(lr_d) + jnp.where(i == 0, MASK_VALUE, 0.0)
        w_d = jnp.exp2(z[:, t:] + suffix(lr_d))
        w_p = jnp.exp2(z[:, :t] + suffix(lr_p) + run_p)
        acc = weighted_values(jnp.concatenate([w_p, w_d], axis=1), vals)
        return qs, run_p + row_total(lr_p), acc

    def earlier_tiles(i, qs, run, acc):
        def cond(c):
            j, live, _, _ = c
            return jnp.logical_and(j >= 0, live)

        def body(c):
            j, _, run, acc = c
            start = pl.multiple_of(j * t, t)
            z = lax.dot_general(qs, k_ref[0, pl.ds(start, t), :], nt_dims,
                                preferred_element_type=F32)
            lr = log_one_minus_sigmoid(z)
            w = jnp.exp2(z + suffix(lr) + run)
            acc = acc + weighted_values(w, v_ref[0, pl.ds(start, t), :])
            run = run + row_total(lr)
            return j - 1, jnp.max(run) >= EXP2_UNDERFLOW, run, acc

        _, _, _, acc = lax.while_loop(
            cond, body, (i - 2, jnp.max(run) >= EXP2_UNDERFLOW, run, acc))
        o_ref[0, pl.ds(pl.multiple_of(i * t, t), t), :] = acc

    def q_tiles(g, carry):
        tiles = [g * SB_UNROLL + s for s in range(SB_UNROLL)]
        scored = [score_diagonal_and_previous(i) for i in tiles]
        heads = [weigh_diagonal_and_previous(i, *s) for i, s in zip(tiles, scored)]
        for i, (qs, run, acc) in zip(tiles, heads):
            earlier_tiles(i, qs, run, acc)
        return carry

    lax.fori_loop(0, n_tiles // SB_UNROLL, q_tiles, 0)

    for src_ref, dst_ref in zip(rest[:n_cast], rest[n_cast + 1:]):
        dst_ref[...] = src_ref[...].astype(BF16)


def _sb_attention(proj, u, weights):
    B, S, _ = proj.shape
    n_pairs = SB_HEADS * HEAD_DIM // LANES
    n_steps = B * n_pairs
    seq = lambda blk: pl.BlockSpec((1, S, LANES), lambda b, p: (b, 0, blk * n_pairs + p))

    def slice_spec(w, axis):
        block = list(w.shape)
        block[axis] = w.shape[axis] // n_steps
        assert block[axis] * n_steps == w.shape[axis] and block[axis] % (LANES if axis else 16) == 0
        step = lambda b, p: b * n_pairs + p
        index_map = (lambda b, p: (0, step(b, p))) if axis else (lambda b, p: (step(b, p), 0))
        return pl.BlockSpec(tuple(block), index_map)

    cast_specs = [slice_spec(w, axis) for w, axis in weights]
    outs = pl.pallas_call(
        _sb_kernel,
        grid=(B, n_pairs),
        in_specs=[seq(0), seq(1), seq(2), pl.BlockSpec(u.shape, lambda b, p: (0, 0))] + cast_specs,
        out_specs=[pl.BlockSpec((1, S, LANES), lambda b, p: (b, 0, p))] + cast_specs,
        out_shape=[jax.ShapeDtypeStruct((B, S, SB_HEADS * HEAD_DIM), F32)]
        + [jax.ShapeDtypeStruct(w.shape, BF16) for w, _ in weights],
        compiler_params=pltpu.CompilerParams(
            dimension_semantics=("parallel", "parallel"), vmem_limit_bytes=VMEM_LIMIT),
        name="sb_attn",
    )(proj, proj, proj, u, *[w for w, _ in weights])
    return outs[0], outs[1:]


def _swa_bias():
    w = WINDOW
    r = np.arange(2 * w)[:, None]
    c = np.arange(4 * w)[None, :]
    dist = (r % w) + w - (c % (2 * w))
    in_band = (dist >= 0) & (dist < w)
    groups = []
    for g in range(SWA_KV_HEADS):
        head = SWA_GROUP * g + 2 * (r // w) + (c // (2 * w))
        slope = np.exp2(-8.0 * (head + 1) / SWA_HEADS)
        groups.append(np.where(in_band, -slope * dist * LOG2_E, MASK_VALUE))
    rest = np.stack(groups)
    first = np.where((c % (2 * w)) >= w, rest, MASK_VALUE)
    return jnp.asarray(np.stack([first, rest]), dtype=F32)


def _swa_kernel(sink_ref, bias_ref, q_ref, kp_ref, kc_ref, vp_ref, vc_ref, o_ref):
    w = WINDOW
    n = pl.program_id(1)
    lane = lax.broadcasted_iota(jnp.int32, (w, LANES), 1)
    lo = lane < HEAD_DIM
    nt_dims = (((1,), (1,)), ((), ()))

    def placed(prev_ref, cur_ref):
        band = jnp.concatenate([prev_ref[0], cur_ref[0]], axis=0).astype(F32)
        swapped = pltpu.roll(band, HEAD_DIM, axis=1)
        in_lo = lax.broadcasted_iota(jnp.int32, band.shape, 1) < HEAD_DIM
        return [
            (jnp.where(in_lo, band, 0.0).astype(BF16), jnp.where(in_lo, 0.0, swapped).astype(BF16)),
            (jnp.where(in_lo, swapped, 0.0).astype(BF16), jnp.where(in_lo, 0.0, band).astype(BF16)),
        ]

    k_placed = placed(kp_ref, kc_ref)
    v_placed = placed(vp_ref, vc_ref)
    band_lo = lax.broadcasted_iota(jnp.int32, (2 * w, LANES), 1) < HEAD_DIM
    ones_lo = jnp.where(band_lo, 1.0, 0.0).astype(BF16)
    ones_hi = jnp.where(band_lo, 0.0, 1.0).astype(BF16)
    variant = jnp.minimum(n, 1)
    for a in range(SWA_TILE // w):
        rows = slice(a * w, (a + 1) * w)
        band = slice(a * w, (a + 2) * w)
        for g in range(SWA_KV_HEADS):
            cols = [slice((2 * g + rb) * LANES, (2 * g + rb + 1) * LANES) for rb in range(2)]
            q2 = jnp.concatenate([q_ref[0, rows, cols[0]], q_ref[0, rows, cols[1]]], axis=0)
            k2 = jnp.concatenate([k_placed[g][0][band], k_placed[g][1][band]], axis=0)
            v2 = jnp.concatenate([jnp.concatenate([v_placed[g][0][band], ones_lo], axis=1),
                                  jnp.concatenate([v_placed[g][1][band], ones_hi], axis=1)], axis=0)
            bias = bias_ref[variant, g] if a == 0 else bias_ref[1, g]
            s = lax.dot_general(q2, k2, nt_dims, preferred_element_type=F32) + bias
            p_rows, sink_terms = [], []
            for rb in range(2):
                p_cols, sink_cols = [], []
                for half in range(2):
                    sink = sink_ref[SWA_GROUP * g + 2 * rb + half] * LOG2_E
                    sq = s[rb * w:(rb + 1) * w, half * 2 * w:(half + 1) * 2 * w]
                    m = jnp.maximum(jnp.max(sq, axis=-1, keepdims=True), sink)
                    p_cols.append(jnp.exp2(sq - m).astype(BF16))
                    sink_cols.append(jnp.exp2(sink - m))
                p_rows.append(jnp.concatenate(p_cols, axis=1))
                sink_terms.append(jnp.where(lo, sink_cols[0], sink_cols[1]))
            o2 = jnp.dot(jnp.concatenate(p_rows, axis=0), v2, preferred_element_type=F32)
            for rb in range(2):
                sub = slice(rb * w, (rb + 1) * w)
                o_ref[0, rows, cols[rb]] = o2[sub, :LANES] / (o2[sub, LANES:] + sink_terms[rb])


def _swa_attention(proj, sinks):
    B, S, _ = proj.shape
    t = SWA_TILE
    w = WINDOW
    q_blk = (3 * SB_HEADS * HEAD_DIM) // (SWA_HEADS * HEAD_DIM)
    k_blk = (3 * SB_HEADS + SWA_HEADS) * HEAD_DIM // LANES
    v_blk = k_blk + 1
    prev = lambda n: jnp.maximum(n * (t // w) - 1, 0)
    bias = _swa_bias()
    return pl.pallas_call(
        _swa_kernel,
        grid=(B, S // t),
        in_specs=[
            pl.BlockSpec(memory_space=pltpu.SMEM),
            pl.BlockSpec(bias.shape, lambda b, n: (0, 0, 0, 0)),
            pl.BlockSpec((1, t, SWA_HEADS * HEAD_DIM), lambda b, n: (b, n, q_blk)),
            pl.BlockSpec((1, w, LANES), lambda b, n: (b, prev(n), k_blk)),
            pl.BlockSpec((1, t, LANES), lambda b, n: (b, n, k_blk)),
            pl.BlockSpec((1, w, LANES), lambda b, n: (b, prev(n), v_blk)),
            pl.BlockSpec((1, t, LANES), lambda b, n: (b, n, v_blk)),
        ],
        out_specs=pl.BlockSpec((1, t, SWA_HEADS * HEAD_DIM), lambda b, n: (b, n, 0)),
        out_shape=jax.ShapeDtypeStruct((B, S, SWA_HEADS * HEAD_DIM), F32),
        compiler_params=pltpu.CompilerParams(
            dimension_semantics=("parallel", "arbitrary"), vmem_limit_bytes=VMEM_LIMIT),
        name="swa_attn",
    )(sinks, bias, proj, proj, proj, proj, proj)


def _ff_chunks(d_ff):
    n_tiles = d_ff // MXU_TILE
    per = FF_CHUNK // MXU_TILE
    sizes = [per] * (n_tiles // per) + ([n_tiles % per] if n_tiles % per else [])
    return tuple(s * MXU_TILE for s in sizes)


def _post_kernel(ysb_ref, ysw_ref, x_ref, mod_ref, wo_ref, gsb_ref, gsw_ref, l1g_ref, l1b_ref,
                 wgu_ref, wd_ref, l2g_ref, l2b_ref, o_ref, *, alpha, chunks):
    sb_w = ysb_ref.shape[-1]
    d_ff = wd_ref.shape[0]
    d = x_ref.shape[-1]
    batch_row = pl.ds(pl.program_id(0), 1)
    mod = lambda k: mod_ref[batch_row, k * d:(k + 1) * d]

    def attention_tail(rows):
        n_sb = _rms_norm(ysb_ref[0, rows, :], gsb_ref[...]).astype(BF16)
        n_sw = _rms_norm(ysw_ref[0, rows, :], gsw_ref[...]).astype(BF16)
        attn = (jnp.dot(n_sb, wo_ref[:sb_w, :], preferred_element_type=F32)
                + jnp.dot(n_sw, wo_ref[sb_w:, :], preferred_element_type=F32))
        return _layer_norm(alpha * x_ref[0, rows, :] + (1.0 + mod(2)) * attn,
                           l1g_ref[...], l1b_ref[...])

    def swiglu(x1):
        h = (x1 * (1.0 + mod(4)) + mod(3)).astype(BF16)
        ffn = None
        start = 0
        for width in chunks:
            gate = jnp.dot(h, wgu_ref[:, start:start + width], preferred_element_type=F32)
            up = jnp.dot(h, wgu_ref[:, d_ff + start:d_ff + start + width],
                         preferred_element_type=F32)
            a = (gate * jax.nn.sigmoid(gate) * up).astype(BF16)
            part = jnp.dot(a, wd_ref[start:start + width, :], preferred_element_type=F32)
            ffn = part if ffn is None else ffn + part
            start += width
        return ffn

    n_rows = x_ref.shape[1]
    groups = [slice(r, r + n_rows // POST_SPLIT) for r in range(0, n_rows, n_rows // POST_SPLIT)]
    x1s = [attention_tail(rows) for rows in groups]
    ffns = [swiglu(x1) for x1 in x1s]
    for rows, x1, ffn in zip(groups, x1s, ffns):
        o_ref[0, rows, :] = _layer_norm(alpha * x1 + (1.0 + mod(5)) * ffn,
                                        l2g_ref[...], l2b_ref[...])


def _post_attention(y_sb, y_sw, x, mod, w_out, gn_sb, gn_swa, ln1_g, ln1_b, w_gu, w_down,
                    ln2_g, ln2_b, alpha):
    B, S, D = x.shape
    W = y_sb.shape[-1]
    d_ff = w_down.shape[0]
    row = lambda b, i: (b, i, 0)
    const2 = lambda b, i: (0, 0)
    resident = lambda a: pl.BlockSpec(a.shape, const2, pipeline_mode=pl.Buffered(1))
    return pl.pallas_call(
        functools.partial(_post_kernel, alpha=alpha, chunks=_ff_chunks(d_ff)),
        grid=(B, S // POST_ROW_TILE),
        in_specs=[
            pl.BlockSpec((1, POST_ROW_TILE, W), row),
            pl.BlockSpec((1, POST_ROW_TILE, W), row),
            pl.BlockSpec((1, POST_ROW_TILE, D), row),
            resident(mod),
            resident(w_out), resident(gn_sb), resident(gn_swa), resident(ln1_g), resident(ln1_b),
            resident(w_gu), resident(w_down), resident(ln2_g), resident(ln2_b),
        ],
        out_specs=pl.BlockSpec((1, POST_ROW_TILE, D), row),
        out_shape=jax.ShapeDtypeStruct((B, S, D), F32),
        compiler_params=pltpu.CompilerParams(
            dimension_semantics=("parallel", "parallel"), vmem_limit_bytes=VMEM_LIMIT),
        name="post_attn",
    )(y_sb, y_sw, x, mod, w_out, gn_sb, gn_swa, ln1_g, ln1_b, w_gu, w_down, ln2_g, ln2_b)


def kernel(x, c, w_ada, b_ada, w_in, b_in, sinks, gn_sb, gn_swa, w_out, ln1_g, ln1_b,
           w_gu, w_down, ln2_g, ln2_b):
    depth = w_ada.shape[0]
    B, S, D = x.shape
    alpha = (2.0 * depth) ** 0.25
    sb_w = SB_HEADS * HEAD_DIM
    swa_qw = SWA_HEADS * HEAD_DIM
    d_in = w_in.shape[-1]
    cols = np.arange(d_in)
    is_q = (cols < sb_w) | ((cols >= 3 * sb_w) & (cols < 3 * sb_w + swa_qw))
    col_scale = jnp.asarray(np.where(is_q, LOG2_E / math.sqrt(HEAD_DIM), 1.0)[None, :], dtype=F32)
    tri = jnp.asarray(np.arange(SB_TILE)[:, None] >= np.arange(SB_TILE)[None, :], dtype=BF16)
    row2 = lambda a: a.reshape(1, -1)

    for l in range(depth):
        mod = _mod(c, w_ada[l], row2(b_ada[l]))
        proj = _in_proj(x, mod, w_in[l], row2(b_in[l]), col_scale)
        y_sb, (w_o, w_g, w_d) = _sb_attention(
            proj, tri, [(w_out[l], 0), (w_gu[l], 0), (w_down[l], 0)])
        y_sw = _swa_attention(proj, sinks[l])
        x = _post_attention(y_sb, y_sw, x, mod, w_o, row2(gn_sb[l]), row2(gn_swa[l]),
                            row2(ln1_g[l]), row2(ln1_b[l]), w_g, w_d, row2(ln2_g[l]),
                            row2(ln2_b[l]), alpha)
    return x
```

```python
import functools
import math

import jax
import jax.numpy as jnp
import numpy as np
from jax import lax
from jax.experimental import pallas as pl
from jax.experimental.pallas import tpu as pltpu

F32 = jnp.float32
BF16 = jnp.bfloat16

HEAD_DIM = 64
SB_HEADS = 8
SWA_HEADS = 8
SWA_KV_HEADS = 2
SWA_GROUP = SWA_HEADS // SWA_KV_HEADS
WINDOW = 128
LN_EPS = 1e-5
RMS_EPS = 1e-6
MASK_VALUE = -1e30
EXP2_UNDERFLOW = -150.0
LOG2_E = 1.4426950408889634
LANES = 128
BF16_SUBLANES = 16

SB_TILE = 256
SB_UNROLL = 2
SWA_TILE = 1024
ROW_TILE = 1024
POST_ROW_TILE = 1024
POST_SPLIT = 4
MXU_TILE = 256
FF_CHUNK = 768
MOD_COLS = 1536
VMEM_LIMIT = 60 * 1024 * 1024


def _layer_norm(r, g, b):
    mu = jnp.mean(r, axis=-1, keepdims=True)
    d = r - mu
    var = jnp.mean(d * d, axis=-1, keepdims=True)
    return d * lax.rsqrt(var + LN_EPS) * g + b


def _rms_norm(y, g):
    ms = jnp.mean(y * y, axis=-1, keepdims=True)
    return y * lax.rsqrt(ms + RMS_EPS) * g


def _mod_kernel(ct_ref, w_ref, b_ref, o_ref):
    ct = ct_ref[...]
    s = ct * jax.nn.sigmoid(ct)
    w = w_ref[...]
    for b in range(ct.shape[1]):
        o_ref[b:b + 1, :] = jnp.sum(w * s[:, b:b + 1], axis=0, keepdims=True) + b_ref[...]


def _mod(c, w_ada, b_ada):
    B, D = c.shape
    N = w_ada.shape[1]
    return pl.pallas_call(
        _mod_kernel,
        grid=(N // MOD_COLS,),
        in_specs=[
            pl.BlockSpec((D, B), lambda n: (0, 0)),
            pl.BlockSpec((D, MOD_COLS), lambda n: (0, n)),
            pl.BlockSpec((1, MOD_COLS), lambda n: (0, n)),
        ],
        out_specs=pl.BlockSpec((B, MOD_COLS), lambda n: (0, n)),
        out_shape=jax.ShapeDtypeStruct((B, N), F32),
        compiler_params=pltpu.CompilerParams(
            dimension_semantics=("arbitrary",), vmem_limit_bytes=VMEM_LIMIT),
        name="mod",
    )(c.T, w_ada, b_ada)


def _inproj_kernel(x_ref, mod_ref, w_ref, b_ref, cs_ref, o_ref, wb_ref, *, col_chunk):
    @pl.when(jnp.logical_and(pl.program_id(0) == 0, pl.program_id(1) == 0))
    def _():
        wb_ref[...] = w_ref[...].astype(BF16)

    d = x_ref.shape[-1]
    batch_row = pl.ds(pl.program_id(0), 1)
    sh = mod_ref[batch_row, 0:d]
    sc = mod_ref[batch_row, d:2 * d]
    h = (x_ref[0] * (1.0 + sc) + sh).astype(BF16)
    n_cols = w_ref.shape[1]
    for n in range(n_cols // col_chunk):
        sl = slice(n * col_chunk, (n + 1) * col_chunk)
        p = jnp.dot(h, wb_ref[:, sl], preferred_element_type=F32)
        o_ref[0, :, sl] = ((p + b_ref[:, sl]) * cs_ref[:, sl]).astype(BF16)


def _in_proj(x, mod, w_in, b_in, col_scale):
    B, S, D = x.shape
    N = w_in.shape[1]
    return pl.pallas_call(
        functools.partial(_inproj_kernel, col_chunk=768),
        grid=(B, S // ROW_TILE),
        in_specs=[
            pl.BlockSpec((1, ROW_TILE, D), lambda b, i: (b, i, 0)),
            pl.BlockSpec(mod.shape, lambda b, i: (0, 0)),
            pl.BlockSpec((D, N), lambda b, i: (0, 0), pipeline_mode=pl.Buffered(1)),
            pl.BlockSpec((1, N), lambda b, i: (0, 0)),
            pl.BlockSpec((1, N), lambda b, i: (0, 0)),
        ],
        out_specs=pl.BlockSpec((1, ROW_TILE, N), lambda b, i: (b, i, 0)),
        out_shape=jax.ShapeDtypeStruct((B, S, N), BF16),
        scratch_shapes=[pltpu.VMEM((D, N), BF16)],
        compiler_params=pltpu.CompilerParams(
            dimension_semantics=("arbitrary", "arbitrary"), vmem_limit_bytes=VMEM_LIMIT),
        name="in_proj",
    )(x, mod, w_in, b_in, col_scale)


def _sb_kernel(q_ref, k_ref, v_ref, u_ref, o_ref):
    t = SB_TILE
    n_tiles = q_ref.shape[1] // t
    u = u_ref[...]
    first_head = lambda rows: lax.broadcasted_iota(jnp.int32, (rows, LANES), 1) < HEAD_DIM
    row = lax.broadcasted_iota(jnp.int32, (2 * t, t), 0)
    col = lax.broadcasted_iota(jnp.int32, (2 * t, t), 1)
    before = col < (row & (t - 1))
    nt_dims = (((1,), (1,)), ((), ()))

    def log_one_minus_sigmoid(z):
        neg_part = jnp.minimum(z, 0.0)
        neg_relu = neg_part - z
        return neg_relu - jnp.log2(1.0 + jnp.exp2(neg_part + neg_relu))

    def suffix(lr):
        return jnp.dot(lr.astype(BF16), u, preferred_element_type=F32)

    def row_total(lr):
        return jnp.sum(lr, axis=-1, keepdims=True)

    def weighted_values(w, vals):
        w = w.astype(BF16)
        zero = jnp.zeros_like(vals)
        in_first = first_head(vals.shape[0])
        w2 = jnp.concatenate([w[:t], w[t:]], axis=1)
        v2 = jnp.concatenate([jnp.where(in_first, vals, zero), jnp.where(in_first, zero, vals)],
                             axis=0)
        return jnp.dot(w2, v2, preferred_element_type=F32)

    def score_diagonal_and_previous(i):
        here = pl.multiple_of(i * t, t)
        prev = pl.multiple_of(jnp.maximum(i - 1, 0) * t, t)
        q = q_ref[0, pl.ds(here, t), :]
        zero = jnp.zeros_like(q)
        in_first = first_head(t)
        qs = jnp.concatenate([jnp.where(in_first, q, zero), jnp.where(in_first, zero, q)], axis=0)

        keys = jnp.concatenate([k_ref[0, pl.ds(prev, t), :], k_ref[0, pl.ds(here, t), :]], axis=0)
        vals = jnp.concatenate([v_ref[0, pl.ds(prev, t), :], v_ref[0, pl.ds(here, t), :]], axis=0)
        z = lax.dot_general(qs, keys, nt_dims, preferred_element_type=F32)
        z = jnp.concatenate([z[:, :t], jnp.where(before, z[:, t:], MASK_VALUE)], axis=1)
        return qs, vals, z, log_one_minus_sigmoid(z)

    def weigh_diagonal_and_previous(i, qs, vals, z, lr):
        lr_p, lr_d = lr[:, :t], lr[:, t:]
        run_p = row_total(lr_d) + jnp.where(i == 0, MASK_VALUE, 0.0)
        w_d = jnp.exp2(z[:, t:] + suffix(lr_d))
        w_p = jnp.exp2(z[:, :t] + suffix(lr_p) + run_p)
        acc = weighted_values(jnp.concatenate([w_p, w_d], axis=1), vals)
        return qs, run_p + row_total(lr_p), acc

    def earlier_tiles(i, qs, run, acc):
        def cond(c):
            j, live, _, _ = c
            return jnp.logical_and(j >= 0, live)

        def body(c):
            j, _, run, acc = c
            start = pl.multiple_of(j * t, t)
            z = lax.dot_general(qs, k_ref[0, pl.ds(start, t), :], nt_dims,
                                preferred_element_type=F32)
            lr = log_one_minus_sigmoid(z)
            w = jnp.exp2(z + suffix(lr) + run)
            acc = acc + weighted_values(w, v_ref[0, pl.ds(start, t), :])
            run = run + row_total(lr)
            return j - 1, jnp.max(run) >= EXP2_UNDERFLOW, run, acc

        _, _, _, acc = lax.while_loop(
            cond, body, (i - 2, jnp.max(run) >= EXP2_UNDERFLOW, run, acc))
        o_ref[0, pl.ds(pl.multiple_of(i * t, t), t), :] = acc

    def q_tiles(g, carry):
        tiles = [g * SB_UNROLL + s for s in range(SB_UNROLL)]
        scored = [score_diagonal_and_previous(i) for i in tiles]
        heads = [weigh_diagonal_and_previous(i, *s) for i, s in zip(tiles, scored)]
        for i, (qs, run, acc) in zip(tiles, heads):
            earlier_tiles(i, qs, run, acc)
        return carry

    lax.fori_loop(0, n_tiles // SB_UNROLL, q_tiles, 0)


def _sb_attention(proj, u):
    B, S, _ = proj.shape
    n_pairs = SB_HEADS * HEAD_DIM // LANES
    seq = lambda blk: pl.BlockSpec((1, S, LANES), lambda b, p: (b, 0, blk * n_pairs + p))
    return pl.pallas_call(
        _sb_kernel,
        grid=(B, n_pairs),
        in_specs=[seq(0), seq(1), seq(2), pl.BlockSpec(u.shape, lambda b, p: (0, 0))],
        out_specs=pl.BlockSpec((1, S, LANES), lambda b, p: (b, 0, p)),
        out_shape=jax.ShapeDtypeStruct((B, S, SB_HEADS * HEAD_DIM), F32),
        compiler_params=pltpu.CompilerParams(
            dimension_semantics=("parallel", "parallel"), vmem_limit_bytes=VMEM_LIMIT),
        name="sb_attn",
    )(proj, proj, proj, u)


def _swa_bias():
    w = WINDOW
    r = np.arange(2 * w)[:, None]
    c = np.arange(4 * w)[None, :]
    dist = (r % w) + w - (c % (2 * w))
    in_band = (dist >= 0) & (dist < w)
    groups = []
    for g in range(SWA_KV_HEADS):
        head = SWA_GROUP * g + 2 * (r // w) + (c // (2 * w))
        slope = np.exp2(-8.0 * (head + 1) / SWA_HEADS)
        groups.append(np.where(in_band, -slope * dist * LOG2_E, MASK_VALUE))
    rest = np.stack(groups)
    first = np.where((c % (2 * w)) >= w, rest, MASK_VALUE)
    return jnp.asarray(np.stack([first, rest]), dtype=F32)


def _swa_kernel(sink_ref, bias_ref, q_ref, kp_ref, kc_ref, vp_ref, vc_ref, *rest):
    w = WINDOW
    n_cast = (len(rest) - 1) // 2
    o_ref = rest[n_cast]
    n = pl.program_id(1)
    lane = lax.broadcasted_iota(jnp.int32, (w, LANES), 1)
    lo = lane < HEAD_DIM
    nt_dims = (((1,), (1,)), ((), ()))

    def placed(prev_ref, cur_ref):
        band = jnp.concatenate([prev_ref[0], cur_ref[0]], axis=0).astype(F32)
        swapped = pltpu.roll(band, HEAD_DIM, axis=1)
        in_lo = lax.broadcasted_iota(jnp.int32, band.shape, 1) < HEAD_DIM
        return [
            (jnp.where(in_lo, band, 0.0).astype(BF16), jnp.where(in_lo, 0.0, swapped).astype(BF16)),
            (jnp.where(in_lo, swapped, 0.0).astype(BF16), jnp.where(in_lo, 0.0, band).astype(BF16)),
        ]

    k_placed = placed(kp_ref, kc_ref)
    v_placed = placed(vp_ref, vc_ref)
    band_lo = lax.broadcasted_iota(jnp.int32, (2 * w, LANES), 1) < HEAD_DIM
    ones_lo = jnp.where(band_lo, 1.0, 0.0).astype(BF16)
    ones_hi = jnp.where(band_lo, 0.0, 1.0).astype(BF16)
    variant = jnp.minimum(n, 1)
    for a in range(SWA_TILE // w):
        rows = slice(a * w, (a + 1) * w)
        band = slice(a * w, (a + 2) * w)
        for g in range(SWA_KV_HEADS):
            cols = [slice((2 * g + rb) * LANES, (2 * g + rb + 1) * LANES) for rb in range(2)]
            q2 = jnp.concatenate([q_ref[0, rows, cols[0]], q_ref[0, rows, cols[1]]], axis=0)
            k2 = jnp.concatenate([k_placed[g][0][band], k_placed[g][1][band]], axis=0)
            v2 = jnp.concatenate([jnp.concatenate([v_placed[g][0][band], ones_lo], axis=1),
                                  jnp.concatenate([v_placed[g][1][band], ones_hi], axis=1)], axis=0)
            bias = bias_ref[variant, g] if a == 0 else bias_ref[1, g]
            s = lax.dot_general(q2, k2, nt_dims, preferred_element_type=F32) + bias
            p_rows, sink_terms = [], []
            for rb in range(2):
                p_cols, sink_cols = [], []
                for half in range(2):
                    sink = sink_ref[SWA_GROUP * g + 2 * rb + half] * LOG2_E
                    sq = s[rb * w:(rb + 1) * w, half * 2 * w:(half + 1) * 2 * w]
                    m = jnp.maximum(jnp.max(sq, axis=-1, keepdims=True), sink)
                    p_cols.append(jnp.exp2(sq - m).astype(BF16))
                    sink_cols.append(jnp.exp2(sink - m))
                p_rows.append(jnp.concatenate(p_cols, axis=1))
                sink_terms.append(jnp.where(lo, sink_cols[0], sink_cols[1]))
            o2 = jnp.dot(jnp.concatenate(p_rows, axis=0), v2, preferred_element_type=F32)
            for rb in range(2):
                sub = slice(rb * w, (rb + 1) * w)
                o_ref[0, rows, cols[rb]] = o2[sub, :LANES] / (o2[sub, LANES:] + sink_terms[rb])

    for src_ref, dst_ref in zip(rest[:n_cast], rest[n_cast + 1:]):
        dst_ref[...] = src_ref[...].astype(BF16)


def _swa_attention(proj, sinks, weights):
    B, S, _ = proj.shape
    t = SWA_TILE
    w = WINDOW
    n_steps = B * (S // t)
    q_blk = (3 * SB_HEADS * HEAD_DIM) // (SWA_HEADS * HEAD_DIM)
    k_blk = (3 * SB_HEADS + SWA_HEADS) * HEAD_DIM // LANES
    v_blk = k_blk + 1
    prev = lambda n: jnp.maximum(n * (t // w) - 1, 0)
    bias = _swa_bias()

    def row_slice(weight):
        rows = weight.shape[0] // n_steps
        assert rows * n_steps == weight.shape[0] and rows % BF16_SUBLANES == 0
        return pl.BlockSpec((rows, weight.shape[1]), lambda b, n: (b * (S // t) + n, 0))

    cast_specs = [row_slice(weight) for weight in weights]
    outs = pl.pallas_call(
        _swa_kernel,
        grid=(B, S // t),
        in_specs=[
            pl.BlockSpec(memory_space=pltpu.SMEM),
            pl.BlockSpec(bias.shape, lambda b, n: (0, 0, 0, 0)),
            pl.BlockSpec((1, t, SWA_HEADS * HEAD_DIM), lambda b, n: (b, n, q_blk)),
            pl.BlockSpec((1, w, LANES), lambda b, n: (b, prev(n), k_blk)),
            pl.BlockSpec((1, t, LANES), lambda b, n: (b, n, k_blk)),
            pl.BlockSpec((1, w, LANES), lambda b, n: (b, prev(n), v_blk)),
            pl.BlockSpec((1, t, LANES), lambda b, n: (b, n, v_blk)),
        ] + cast_specs,
        out_specs=[pl.BlockSpec((1, t, SWA_HEADS * HEAD_DIM), lambda b, n: (b, n, 0))] + cast_specs,
        out_shape=[jax.ShapeDtypeStruct((B, S, SWA_HEADS * HEAD_DIM), F32)]
        + [jax.ShapeDtypeStruct(weight.shape, BF16) for weight in weights],
        compiler_params=pltpu.CompilerParams(
            dimension_semantics=("parallel", "arbitrary"), vmem_limit_bytes=VMEM_LIMIT),
        name="swa_attn",
    )(sinks, bias, proj, proj, proj, proj, proj, *weights)
    return outs[0], outs[1:]


def _ff_chunks(d_ff):
    n_tiles = d_ff // MXU_TILE
    per = FF_CHUNK // MXU_TILE
    sizes = [per] * (n_tiles // per) + ([n_tiles % per] if n_tiles % per else [])
    return tuple(s * MXU_TILE for s in sizes)


def _post_kernel(ysb_ref, ysw_ref, x_ref, mod_ref, wo_ref, gsb_ref, gsw_ref, l1g_ref, l1b_ref,
                 wgu_ref, wd_ref, l2g_ref, l2b_ref, o_ref, *, alpha, chunks):
    sb_w = ysb_ref.shape[-1]
    d_ff = wd_ref.shape[0]
    d = x_ref.shape[-1]
    batch_row = pl.ds(pl.program_id(0), 1)
    mod = lambda k: mod_ref[batch_row, k * d:(k + 1) * d]

    def attention_tail(rows):
        n_sb = _rms_norm(ysb_ref[0, rows, :], gsb_ref[...]).astype(BF16)
        n_sw = _rms_norm(ysw_ref[0, rows, :], gsw_ref[...]).astype(BF16)
        attn = (jnp.dot(n_sb, wo_ref[:sb_w, :], preferred_element_type=F32)
                + jnp.dot(n_sw, wo_ref[sb_w:, :], preferred_element_type=F32))
        return _layer_norm(alpha * x_ref[0, rows, :] + (1.0 + mod(2)) * attn,
                           l1g_ref[...], l1b_ref[...])

    def swiglu(x1):
        h = (x1 * (1.0 + mod(4)) + mod(3)).astype(BF16)
        ffn = None
        start = 0
        for width in chunks:
            gate = jnp.dot(h, wgu_ref[:, start:start + width], preferred_element_type=F32)
            up = jnp.dot(h, wgu_ref[:, d_ff + start:d_ff + start + width],
                         preferred_element_type=F32)
            a = (gate * jax.nn.sigmoid(gate) * up).astype(BF16)
            part = jnp.dot(a, wd_ref[start:start + width, :], preferred_element_type=F32)
            ffn = part if ffn is None else ffn + part
            start += width
        return ffn

    n_rows = x_ref.shape[1]
    groups = [slice(r, r + n_rows // POST_SPLIT) for r in range(0, n_rows, n_rows // POST_SPLIT)]
    x1s = [attention_tail(rows) for rows in groups]
    ffns = [swiglu(x1) for x1 in x1s]
    for rows, x1, ffn in zip(groups, x1s, ffns):
        o_ref[0, rows, :] = _layer_norm(alpha * x1 + (1.0 + mod(5)) * ffn,
                                        l2g_ref[...], l2b_ref[...])


def _post_attention(y_sb, y_sw, x, mod, w_out, gn_sb, gn_swa, ln1_g, ln1_b, w_gu, w_down,
                    ln2_g, ln2_b, alpha):
    B, S, D = x.shape
    W = y_sb.shape[-1]
    d_ff = w_down.shape[0]
    row = lambda b, i: (b, i, 0)
    const2 = lambda b, i: (0, 0)
    resident = lambda a: pl.BlockSpec(a.shape, const2, pipeline_mode=pl.Buffered(1))
    return pl.pallas_call(
        functools.partial(_post_kernel, alpha=alpha, chunks=_ff_chunks(d_ff)),
        grid=(B, S // POST_ROW_TILE),
        in_specs=[
            pl.BlockSpec((1, POST_ROW_TILE, W), row),
            pl.BlockSpec((1, POST_ROW_TILE, W), row),
            pl.BlockSpec((1, POST_ROW_TILE, D), row),
            resident(mod),
            resident(w_out), resident(gn_sb), resident(gn_swa), resident(ln1_g), resident(ln1_b),
            resident(w_gu), resident(w_down), resident(ln2_g), resident(ln2_b),
        ],
        out_specs=pl.BlockSpec((1, POST_ROW_TILE, D), row),
        out_shape=jax.ShapeDtypeStruct((B, S, D), F32),
        compiler_params=pltpu.CompilerParams(
            dimension_semantics=("parallel", "parallel"), vmem_limit_bytes=VMEM_LIMIT),
        name="post_attn",
    )(y_sb, y_sw, x, mod, w_out, gn_sb, gn_swa, ln1_g, ln1_b, w_gu, w_down, ln2_g, ln2_b)


def kernel(x, c, w_ada, b_ada, w_in, b_in, sinks, gn_sb, gn_swa, w_out, ln1_g, ln1_b,
           w_gu, w_down, ln2_g, ln2_b):
    depth = w_ada.shape[0]
    B, S, D = x.shape
    alpha = (2.0 * depth) ** 0.25
    sb_w = SB_HEADS * HEAD_DIM
    swa_qw = SWA_HEADS * HEAD_DIM
    d_in = w_in.shape[-1]
    cols = np.arange(d_in)
    is_q = (cols < sb_w) | ((cols >= 3 * sb_w) & (cols < 3 * sb_w + swa_qw))
    col_scale = jnp.asarray(np.where(is_q, LOG2_E / math.sqrt(HEAD_DIM), 1.0)[None, :], dtype=F32)
    tri = jnp.asarray(np.arange(SB_TILE)[:, None] >= np.arange(SB_TILE)[None, :], dtype=BF16)
    row2 = lambda a: a.reshape(1, -1)

    for l in range(depth):
        mod = _mod(c, w_ada[l], row2(b_ada[l]))
        proj = _in_proj(x, mod, w_in[l], row2(b_in[l]), col_scale)
        y_sb = _sb_attention(proj, tri)
        y_sw, (w_o, w_g, w_d) = _swa_attention(proj, sinks[l], [w_out[l], w_gu[l], w_down[l]])
        x = _post_attention(y_sb, y_sw, x, mod, w_o, row2(gn_sb[l]), row2(gn_swa[l]),
                            row2(ln1_g[l]), row2(ln1_b[l]), w_g, w_d, row2(ln2_g[l]),
                            row2(ln2_b[l]), alpha)
    return x
```

```python
import functools
import math

import jax
import jax.numpy as jnp
import numpy as np
from jax import lax
from jax.experimental import pallas as pl
from jax.experimental.pallas import tpu as pltpu

F32 = jnp.float32
BF16 = jnp.bfloat16

HEAD_DIM = 64
SB_HEADS = 8
SWA_HEADS = 8
SWA_KV_HEADS = 2
SWA_GROUP = SWA_HEADS // SWA_KV_HEADS
WINDOW = 128
LN_EPS = 1e-5
RMS_EPS = 1e-6
MASK_VALUE = -1e30
EXP2_UNDERFLOW = -150.0
LOG2_E = 1.4426950408889634
LANES = 128
BF16_SUBLANES = 16

SB_TILE = 256
SB_UNROLL = 2
SWA_TILE = 1024
ROW_TILE = 1024
POST_ROW_TILE = 1024
POST_SPLIT = 4
MXU_TILE = 256
FF_CHUNK = 768
MOD_COLS = 1536
VMEM_LIMIT = 60 * 1024 * 1024


def _layer_norm(r, g, b):
    mu = jnp.mean(r, axis=-1, keepdims=True)
    d = r - mu
    var = jnp.mean(d * d, axis=-1, keepdims=True)
    return d * lax.rsqrt(var + LN_EPS) * g + b


def _rms_norm(y, g):
    ms = jnp.mean(y * y, axis=-1, keepdims=True)
    return y * lax.rsqrt(ms + RMS_EPS) * g


def _mod_kernel(ct_ref, wa_ref, wb_ref, b_ref, o_ref):
    ct = ct_ref[...]
    s = ct * jax.nn.sigmoid(ct)
    half = wa_ref.shape[0]
    for b in range(ct.shape[1]):
        col = s[:, b:b + 1]
        o_ref[b:b + 1, :] = (jnp.sum(wa_ref[...] * col[:half], axis=0, keepdims=True)
                             + jnp.sum(wb_ref[...] * col[half:], axis=0, keepdims=True)
                             + b_ref[...])


def _mod(c, w_ada, b_ada):
    B, D = c.shape
    N = w_ada.shape[1]
    return pl.pallas_call(
        _mod_kernel,
        grid=(N // MOD_COLS,),
        in_specs=[
            pl.BlockSpec((D, B), lambda n: (0, 0)),
            pl.BlockSpec((D // 2, MOD_COLS), lambda n: (0, n)),
            pl.BlockSpec((D // 2, MOD_COLS), lambda n: (1, n)),
            pl.BlockSpec((1, MOD_COLS), lambda n: (0, n)),
        ],
        out_specs=pl.BlockSpec((B, MOD_COLS), lambda n: (0, n)),
        out_shape=jax.ShapeDtypeStruct((B, N), F32),
        compiler_params=pltpu.CompilerParams(
            dimension_semantics=("arbitrary",), vmem_limit_bytes=VMEM_LIMIT),
        name="mod",
    )(c.T, w_ada, w_ada, b_ada)


def _inproj_kernel(x_ref, mod_ref, w_ref, b_ref, cs_ref, o_ref, wb_ref, *, col_chunk):
    @pl.when(jnp.logical_and(pl.program_id(0) == 0, pl.program_id(1) == 0))
    def _():
        wb_ref[...] = w_ref[...].astype(BF16)

    d = x_ref.shape[-1]
    batch_row = pl.ds(pl.program_id(0), 1)
    sh = mod_ref[batch_row, 0:d]
    sc = mod_ref[batch_row, d:2 * d]
    h = (x_ref[0] * (1.0 + sc) + sh).astype(BF16)
    n_cols = w_ref.shape[1]
    for n in range(n_cols // col_chunk):
        sl = slice(n * col_chunk, (n + 1) * col_chunk)
        p = jnp.dot(h, wb_ref[:, sl], preferred_element_type=F32)
        o_ref[0, :, sl] = ((p + b_ref[:, sl]) * cs_ref[:, sl]).astype(BF16)


def _in_proj(x, mod, w_in, b_in, col_scale):
    B, S, D = x.shape
    N = w_in.shape[1]
    return pl.pallas_call(
        functools.partial(_inproj_kernel, col_chunk=768),
        grid=(B, S // ROW_TILE),
        in_specs=[
            pl.BlockSpec((1, ROW_TILE, D), lambda b, i: (b, i, 0)),
            pl.BlockSpec(mod.shape, lambda b, i: (0, 0)),
            pl.BlockSpec((D, N), lambda b, i: (0, 0), pipeline_mode=pl.Buffered(1)),
            pl.BlockSpec((1, N), lambda b, i: (0, 0)),
            pl.BlockSpec((1, N), lambda b, i: (0, 0)),
        ],
        out_specs=pl.BlockSpec((1, ROW_TILE, N), lambda b, i: (b, i, 0)),
        out_shape=jax.ShapeDtypeStruct((B, S, N), BF16),
        scratch_shapes=[pltpu.VMEM((D, N), BF16)],
        compiler_params=pltpu.CompilerParams(
            dimension_semantics=("arbitrary", "arbitrary"), vmem_limit_bytes=VMEM_LIMIT),
        name="in_proj",
    )(x, mod, w_in, b_in, col_scale)


def _sb_kernel(q_ref, k_ref, v_ref, u_ref, o_ref):
    t = SB_TILE
    n_tiles = q_ref.shape[1] // t
    u = u_ref[...]
    first_head = lambda rows: lax.broadcasted_iota(jnp.int32, (rows, LANES), 1) < HEAD_DIM
    row = lax.broadcasted_iota(jnp.int32, (2 * t, t), 0)
    col = lax.broadcasted_iota(jnp.int32, (2 * t, t), 1)
    before = col < (row & (t - 1))
    nt_dims = (((1,), (1,)), ((), ()))

    def log_one_minus_sigmoid(z):
        neg_part = jnp.minimum(z, 0.0)
        neg_relu = neg_part - z
        return neg_relu - jnp.log2(1.0 + jnp.exp2(neg_part + neg_relu))

    def suffix(lr):
        return jnp.dot(lr.astype(BF16), u, preferred_element_type=F32)

    def row_total(lr):
        return jnp.sum(lr, axis=-1, keepdims=True)

    def weighted_values(w, vals):
        w = w.astype(BF16)
        zero = jnp.zeros_like(vals)
        in_first = first_head(vals.shape[0])
        w2 = jnp.concatenate([w[:t], w[t:]], axis=1)
        v2 = jnp.concatenate([jnp.where(in_first, vals, zero), jnp.where(in_first, zero, vals)],
                             axis=0)
        return jnp.dot(w2, v2, preferred_element_type=F32)

    def score_diagonal_and_previous(i):
        here = pl.multiple_of(i * t, t)
        prev = pl.multiple_of(jnp.maximum(i - 1, 0) * t, t)
        q = q_ref[0, pl.ds(here, t), :]
        zero = jnp.zeros_like(q)
        in_first = first_head(t)
        qs = jnp.concatenate([jnp.where(in_first, q, zero), jnp.where(in_first, zero, q)], axis=0)

        keys = jnp.concatenate([k_ref[0, pl.ds(prev, t), :], k_ref[0, pl.ds(here, t), :]], axis=0)
        vals = jnp.concatenate([v_ref[0, pl.ds(prev, t), :], v_ref[0, pl.ds(here, t), :]], axis=0)
        z = lax.dot_general(qs, keys, nt_dims, preferred_element_type=F32)
        return qs, vals, z, log_one_minus_sigmoid(z)

    def weigh_diagonal_and_previous(i, qs, vals, z, lr):
        lr_p, lr_d = lr[:, :t], jnp.where(before, lr[:, t:], 0.0)
        run_p = row_total(lr_d) + jnp.where(i == 0, MASK_VALUE, 0.0)
        w_d = jnp.where(before, jnp.exp2(z[:, t:] + suffix(lr_d)), 0.0)
        w_p = jnp.exp2(z[:, :t] + suffix(lr_p) + run_p)
        acc = weighted_values(jnp.concatenate([w_p, w_d], axis=1), vals)
        run = run_p + row_total(lr_p)
        return qs, run, acc, jnp.max(run) >= EXP2_UNDERFLOW

    def earlier_tiles(i, qs, run, acc, live):
        def cond(c):
            j, live, _, _ = c
            return jnp.logical_and(j >= 0, live)

        def body(c):
            j, _, run, acc = c
            start = pl.multiple_of(j * t, t)
            z = lax.dot_general(qs, k_ref[0, pl.ds(start, t), :], nt_dims,
                                preferred_element_type=F32)
            lr = log_one_minus_sigmoid(z)
            w = jnp.exp2(z + suffix(lr) + run)
            acc = acc + weighted_values(w, v_ref[0, pl.ds(start, t), :])
            run = run + row_total(lr)
            return j - 1, jnp.max(run) >= EXP2_UNDERFLOW, run, acc

        _, _, _, acc = lax.while_loop(cond, body, (i - 2, live, run, acc))
        o_ref[0, pl.ds(pl.multiple_of(i * t, t), t), :] = acc

    def q_tiles(g, carry):
        tiles = [g * SB_UNROLL + s for s in range(SB_UNROLL)]
        heads = [weigh_diagonal_and_previous(i, *score_diagonal_and_previous(i)) for i in tiles]
        for i, head in zip(tiles, heads):
            earlier_tiles(i, *head)
        return carry

    lax.fori_loop(0, n_tiles // SB_UNROLL, q_tiles, 0)


def _sb_attention(proj, u):
    B, S, _ = proj.shape
    n_pairs = SB_HEADS * HEAD_DIM // LANES
    seq = lambda blk: pl.BlockSpec((1, S, LANES), lambda b, p: (b, 0, blk * n_pairs + p))
    return pl.pallas_call(
        _sb_kernel,
        grid=(B, n_pairs),
        in_specs=[seq(0), seq(1), seq(2), pl.BlockSpec(u.shape, lambda b, p: (0, 0))],
        out_specs=pl.BlockSpec((1, S, LANES), lambda b, p: (b, 0, p)),
        out_shape=jax.ShapeDtypeStruct((B, S, SB_HEADS * HEAD_DIM), F32),
        compiler_params=pltpu.CompilerParams(
            dimension_semantics=("parallel", "parallel"), vmem_limit_bytes=VMEM_LIMIT),
        name="sb_attn",
    )(proj, proj, proj, u)


def _swa_bias():
    w = WINDOW
    r = np.arange(2 * w)[:, None]
    c = np.arange(4 * w)[None, :]
    dist = (r % w) + w - (c % (2 * w))
    in_band = (dist >= 0) & (dist < w)
    groups = []
    for g in range(SWA_KV_HEADS):
        head = SWA_GROUP * g + 2 * (r // w) + (c // (2 * w))
        slope = np.exp2(-8.0 * (head + 1) / SWA_HEADS)
        groups.append(np.where(in_band, -slope * dist * LOG2_E, MASK_VALUE))
    rest = np.stack(groups)
    first = np.where((c % (2 * w)) >= w, rest, MASK_VALUE)
    return jnp.asarray(np.stack([first, rest]), dtype=F32)


def _swa_kernel(sink_ref, bias_ref, q_ref, kp_ref, kc_ref, vp_ref, vc_ref, *rest):
    w = WINDOW
    n_cast = (len(rest) - 1) // 2
    o_ref = rest[n_cast]
    n = pl.program_id(1)
    lane = lax.broadcasted_iota(jnp.int32, (w, LANES), 1)
    lo = lane < HEAD_DIM
    nt_dims = (((1,), (1,)), ((), ()))

    def placed(prev_ref, cur_ref):
        band = jnp.concatenate([prev_ref[0], cur_ref[0]], axis=0).astype(F32)
        swapped = pltpu.roll(band, HEAD_DIM, axis=1)
        in_lo = lax.broadcasted_iota(jnp.int32, band.shape, 1) < HEAD_DIM
        return [
            (jnp.where(in_lo, band, 0.0).astype(BF16), jnp.where(in_lo, 0.0, swapped).astype(BF16)),
            (jnp.where(in_lo, swapped, 0.0).astype(BF16), jnp.where(in_lo, 0.0, band).astype(BF16)),
        ]

    k_placed = placed(kp_ref, kc_ref)
    v_placed = placed(vp_ref, vc_ref)
    band_lo = lax.broadcasted_iota(jnp.int32, (2 * w, LANES), 1) < HEAD_DIM
    ones_lo = jnp.where(band_lo, 1.0, 0.0).astype(BF16)
    ones_hi = jnp.where(band_lo, 0.0, 1.0).astype(BF16)
    variant = jnp.minimum(n, 1)
    for a in range(SWA_TILE // w):
        rows = slice(a * w, (a + 1) * w)
        band = slice(a * w, (a + 2) * w)
        for g in range(SWA_KV_HEADS):
            cols = [slice((2 * g + rb) * LANES, (2 * g + rb + 1) * LANES) for rb in range(2)]
            q2 = jnp.concatenate([q_ref[0, rows, cols[0]], q_ref[0, rows, cols[1]]], axis=0)
            k2 = jnp.concatenate([k_placed[g][0][band], k_placed[g][1][band]], axis=0)
            v2 = jnp.concatenate([jnp.concatenate([v_placed[g][0][band], ones_lo], axis=1),
                                  jnp.concatenate([v_placed[g][1][band], ones_hi], axis=1)], axis=0)
            bias = bias_ref[variant, g] if a == 0 else bias_ref[1, g]
            s = lax.dot_general(q2, k2, nt_dims, preferred_element_type=F32) + bias
            p_rows, sink_terms = [], []
            for rb in range(2):
                p_cols, sink_cols = [], []
                for half in range(2):
                    sink = sink_ref[SWA_GROUP * g + 2 * rb + half] * LOG2_E
                    sq = s[rb * w:(rb + 1) * w, half * 2 * w:(half + 1) * 2 * w]
                    m = jnp.maximum(jnp.max(sq, axis=-1, keepdims=True), sink)
                    p_cols.append(jnp.exp2(sq - m).astype(BF16))
                    sink_cols.append(jnp.exp2(sink - m))
                p_rows.append(jnp.concatenate(p_cols, axis=1))
                sink_terms.append(jnp.where(lo, sink_cols[0], sink_cols[1]))
            o2 = jnp.dot(jnp.concatenate(p_rows, axis=0), v2, preferred_element_type=F32)
            for rb in range(2):
                sub = slice(rb * w, (rb + 1) * w)
                o_ref[0, rows, cols[rb]] = o2[sub, :LANES] / (o2[sub, LANES:] + sink_terms[rb])

    for src_ref, dst_ref in zip(rest[:n_cast], rest[n_cast + 1:]):
        dst_ref[...] = src_ref[...].astype(BF16)


def _swa_attention(proj, sinks, weights):
    B, S, _ = proj.shape
    t = SWA_TILE
    w = WINDOW
    n_steps = B * (S // t)
    q_blk = (3 * SB_HEADS * HEAD_DIM) // (SWA_HEADS * HEAD_DIM)
    k_blk = (3 * SB_HEADS + SWA_HEADS) * HEAD_DIM // LANES
    v_blk = k_blk + 1
    prev = lambda n: jnp.maximum(n * (t // w) - 1, 0)
    bias = _swa_bias()

    def row_slice(weight):
        rows = weight.shape[0] // n_steps
        assert rows * n_steps == weight.shape[0] and rows % BF16_SUBLANES == 0
        return pl.BlockSpec((rows, weight.shape[1]), lambda b, n: (b * (S // t) + n, 0))

    cast_specs = [row_slice(weight) for weight in weights]
    outs = pl.pallas_call(
        _swa_kernel,
        grid=(B, S // t),
        in_specs=[
            pl.BlockSpec(memory_space=pltpu.SMEM),
            pl.BlockSpec(bias.shape, lambda b, n: (0, 0, 0, 0)),
            pl.BlockSpec((1, t, SWA_HEADS * HEAD_DIM), lambda b, n: (b, n, q_blk)),
            pl.BlockSpec((1, w, LANES), lambda b, n: (b, prev(n), k_blk)),
            pl.BlockSpec((1, t, LANES), lambda b, n: (b, n, k_blk)),
            pl.BlockSpec((1, w, LANES), lambda b, n: (b, prev(n), v_blk)),
            pl.BlockSpec((1, t, LANES), lambda b, n: (b, n, v_blk)),
        ] + cast_specs,
        out_specs=[pl.BlockSpec((1, t, SWA_HEADS * HEAD_DIM), lambda b, n: (b, n, 0))] + cast_specs,
        out_shape=[jax.ShapeDtypeStruct((B, S, SWA_HEADS * HEAD_DIM), F32)]
        + [jax.ShapeDtypeStruct(weight.shape, BF16) for weight in weights],
        compiler_params=pltpu.CompilerParams(
            dimension_semantics=("parallel", "arbitrary"), vmem_limit_bytes=VMEM_LIMIT),
        name="swa_attn",
    )(sinks, bias, proj, proj, proj, proj, proj, *weights)
    return outs[0], outs[1:]


def _ff_chunks(d_ff):
    n_tiles = d_ff // MXU_TILE
    per = FF_CHUNK // MXU_TILE
    sizes = [per] * (n_tiles // per) + ([n_tiles % per] if n_tiles % per else [])
    return tuple(s * MXU_TILE for s in sizes)


def _post_kernel(ysb_ref, ysw_ref, x_ref, mod_ref, wo_ref, gsb_ref, gsw_ref, l1g_ref, l1b_ref,
                 wgu_ref, wd_ref, l2g_ref, l2b_ref, o_ref, *, alpha, chunks):
    sb_w = ysb_ref.shape[-1]
    d_ff = wd_ref.shape[0]
    d = x_ref.shape[-1]
    batch_row = pl.ds(pl.program_id(0), 1)
    mod = lambda k: mod_ref[batch_row, k * d:(k + 1) * d]

    def attention_tail(rows):
        n_sb = _rms_norm(ysb_ref[0, rows, :], gsb_ref[...]).astype(BF16)
        n_sw = _rms_norm(ysw_ref[0, rows, :], gsw_ref[...]).astype(BF16)
        attn = (jnp.dot(n_sb, wo_ref[:sb_w, :], preferred_element_type=F32)
                + jnp.dot(n_sw, wo_ref[sb_w:, :], preferred_element_type=F32))
        return _layer_norm(alpha * x_ref[0, rows, :] + (1.0 + mod(2)) * attn,
                           l1g_ref[...], l1b_ref[...])

    def swiglu(x1):
        h = (x1 * (1.0 + mod(4)) + mod(3)).astype(BF16)
        ffn = None
        start = 0
        for width in chunks:
            gate = jnp.dot(h, wgu_ref[:, start:start + width], preferred_element_type=F32)
            up = jnp.dot(h, wgu_ref[:, d_ff + start:d_ff + start + width],
                         preferred_element_type=F32)
            a = (gate * jax.nn.sigmoid(gate) * up).astype(BF16)
            part = jnp.dot(a, wd_ref[start:start + width, :], preferred_element_type=F32)
            ffn = part if ffn is None else ffn + part
            start += width
        return ffn

    n_rows = x_ref.shape[1]
    groups = [slice(r, r + n_rows // POST_SPLIT) for r in range(0, n_rows, n_rows // POST_SPLIT)]
    x1s = [attention_tail(rows) for rows in groups]
    ffns = [swiglu(x1) for x1 in x1s]
    for rows, x1, ffn in zip(groups, x1s, ffns):
        o_ref[0, rows, :] = _layer_norm(alpha * x1 + (1.0 + mod(5)) * ffn,
                                        l2g_ref[...], l2b_ref[...])


def _post_attention(y_sb, y_sw, x, mod, w_out, gn_sb, gn_swa, ln1_g, ln1_b, w_gu, w_down,
                    ln2_g, ln2_b, alpha):
    B, S, D = x.shape
    W = y_sb.shape[-1]
    d_ff = w_down.shape[0]
    row = lambda b, i: (b, i, 0)
    const2 = lambda b, i: (0, 0)
    resident = lambda a: pl.BlockSpec(a.shape, const2, pipeline_mode=pl.Buffered(1))
    return pl.pallas_call(
        functools.partial(_post_kernel, alpha=alpha, chunks=_ff_chunks(d_ff)),
        grid=(B, S // POST_ROW_TILE),
        in_specs=[
            pl.BlockSpec((1, POST_ROW_TILE, W), row),
            pl.BlockSpec((1, POST_ROW_TILE, W), row),
            pl.BlockSpec((1, POST_ROW_TILE, D), row),
            resident(mod),
            resident(w_out), resident(gn_sb), resident(gn_swa), resident(ln1_g), resident(ln1_b),
            resident(w_gu), resident(w_down), resident(ln2_g), resident(ln2_b),
        ],
        out_specs=pl.BlockSpec((1, POST_ROW_TILE, D), row),
        out_shape=jax.ShapeDtypeStruct((B, S, D), F32),
        compiler_params=pltpu.CompilerParams(
            dimension_semantics=("parallel", "parallel"), vmem_limit_bytes=VMEM_LIMIT),
        name="post_attn",
    )(y_sb, y_sw, x, mod, w_out, gn_sb, gn_swa, ln1_g, ln1_b, w_gu, w_down, ln2_g, ln2_b)


def kernel(x, c, w_ada, b_ada, w_in, b_in, sinks, gn_sb, gn_swa, w_out, ln1_g, ln1_b,
           w_gu, w_down, ln2_g, ln2_b):
    depth = w_ada.shape[0]
    B, S, D = x.shape
    alpha = (2.0 * depth) ** 0.25
    sb_w = SB_HEADS * HEAD_DIM
    swa_qw = SWA_HEADS * HEAD_DIM
    d_in = w_in.shape[-1]
    cols = np.arange(d_in)
    is_q = (cols < sb_w) | ((cols >= 3 * sb_w) & (cols < 3 * sb_w + swa_qw))
    col_scale = jnp.asarray(np.where(is_q, LOG2_E / math.sqrt(HEAD_DIM), 1.0)[None, :], dtype=F32)
    tri = jnp.asarray(np.arange(SB_TILE)[:, None] >= np.arange(SB_TILE)[None, :], dtype=BF16)
    row2 = lambda a: a.reshape(1, -1)

    for l in range(depth):
        mod = _mod(c, w_ada[l], row2(b_ada[l]))
        proj = _in_proj(x, mod, w_in[l], row2(b_in[l]), col_scale)
        y_sb = _sb_attention(proj, tri)
        y_sw, (w_o, w_g, w_d) = _swa_attention(proj, sinks[l], [w_out[l], w_gu[l], w_down[l]])
        x = _post_attention(y_sb, y_sw, x, mod, w_o, row2(gn_sb[l]), row2(gn_swa[l]),
                            row2(ln1_g[l]), row2(ln1_b[l]), w_g, w_d, row2(ln2_g[l]),
                            row2(ln2_b[l]), alpha)
    return x
```

```python
import functools
import math

import jax
import jax.numpy as jnp
import numpy as np
from jax import lax
from jax.experimental import pallas as pl
from jax.experimental.pallas import tpu as pltpu

F32 = jnp.float32
BF16 = jnp.bfloat16

HEAD_DIM = 64
SB_HEADS = 8
SWA_HEADS = 8
SWA_KV_HEADS = 2
SWA_GROUP = SWA_HEADS // SWA_KV_HEADS
WINDOW = 128
LN_EPS = 1e-5
RMS_EPS = 1e-6
MASK_VALUE = -1e30
EXP2_UNDERFLOW = -150.0
LOG2_E = 1.4426950408889634
LANES = 128
BF16_SUBLANES = 16

SB_TILE = 256
SB_UNROLL = 4
SWA_TILE = 1024
ROW_TILE = 1024
POST_ROW_TILE = 1024
POST_SPLIT = 4
MXU_TILE = 256
FF_CHUNK = 768
MOD_COLS = 1536
VMEM_LIMIT = 60 * 1024 * 1024


def _layer_norm(r, g, b):
    mu = jnp.mean(r, axis=-1, keepdims=True)
    d = r - mu
    var = jnp.mean(d * d, axis=-1, keepdims=True)
    return d * lax.rsqrt(var + LN_EPS) * g + b


def _rms_norm(y, g):
    ms = jnp.mean(y * y, axis=-1, keepdims=True)
    return y * lax.rsqrt(ms + RMS_EPS) * g


def _mod_kernel(ct_ref, wa_ref, wb_ref, b_ref, o_ref):
    ct = ct_ref[...]
    s = ct * jax.nn.sigmoid(ct)
    half = wa_ref.shape[0]
    for b in range(ct.shape[1]):
        col = s[:, b:b + 1]
        o_ref[b:b + 1, :] = (jnp.sum(wa_ref[...] * col[:half], axis=0, keepdims=True)
                             + jnp.sum(wb_ref[...] * col[half:], axis=0, keepdims=True)
                             + b_ref[...])


def _mod(c, w_ada, b_ada):
    B, D = c.shape
    N = w_ada.shape[1]
    return pl.pallas_call(
        _mod_kernel,
        grid=(N // MOD_COLS,),
        in_specs=[
            pl.BlockSpec((D, B), lambda n: (0, 0)),
            pl.BlockSpec((D // 2, MOD_COLS), lambda n: (0, n)),
            pl.BlockSpec((D // 2, MOD_COLS), lambda n: (1, n)),
            pl.BlockSpec((1, MOD_COLS), lambda n: (0, n)),
        ],
        out_specs=pl.BlockSpec((B, MOD_COLS), lambda n: (0, n)),
        out_shape=jax.ShapeDtypeStruct((B, N), F32),
        compiler_params=pltpu.CompilerParams(
            dimension_semantics=("arbitrary",), vmem_limit_bytes=VMEM_LIMIT),
        name="mod",
    )(c.T, w_ada, w_ada, b_ada)


def _inproj_kernel(x_ref, mod_ref, w_ref, b_ref, cs_ref, o_ref, wb_ref, *, col_chunk):
    @pl.when(jnp.logical_and(pl.program_id(0) == 0, pl.program_id(1) == 0))
    def _():
        wb_ref[...] = w_ref[...].astype(BF16)

    d = x_ref.shape[-1]
    batch_row = pl.ds(pl.program_id(0), 1)
    sh = mod_ref[batch_row, 0:d]
    sc = mod_ref[batch_row, d:2 * d]
    h = (x_ref[0] * (1.0 + sc) + sh).astype(BF16)
    n_cols = w_ref.shape[1]
    for n in range(n_cols // col_chunk):
        sl = slice(n * col_chunk, (n + 1) * col_chunk)
        p = jnp.dot(h, wb_ref[:, sl], preferred_element_type=F32)
        o_ref[0, :, sl] = ((p + b_ref[:, sl]) * cs_ref[:, sl]).astype(BF16)


def _in_proj(x, mod, w_in, b_in, col_scale):
    B, S, D = x.shape
    N = w_in.shape[1]
    return pl.pallas_call(
        functools.partial(_inproj_kernel, col_chunk=768),
        grid=(B, S // ROW_TILE),
        in_specs=[
            pl.BlockSpec((1, ROW_TILE, D), lambda b, i: (b, i, 0)),
            pl.BlockSpec(mod.shape, lambda b, i: (0, 0)),
            pl.BlockSpec((D, N), lambda b, i: (0, 0), pipeline_mode=pl.Buffered(1)),
            pl.BlockSpec((1, N), lambda b, i: (0, 0)),
            pl.BlockSpec((1, N), lambda b, i: (0, 0)),
        ],
        out_specs=pl.BlockSpec((1, ROW_TILE, N), lambda b, i: (b, i, 0)),
        out_shape=jax.ShapeDtypeStruct((B, S, N), BF16),
        scratch_shapes=[pltpu.VMEM((D, N), BF16)],
        compiler_params=pltpu.CompilerParams(
            dimension_semantics=("arbitrary", "arbitrary"), vmem_limit_bytes=VMEM_LIMIT),
        name="in_proj",
    )(x, mod, w_in, b_in, col_scale)


def _sb_kernel(q_ref, k_ref, v_ref, u_ref, o_ref):
    t = SB_TILE
    n_tiles = q_ref.shape[1] // t
    u = u_ref[...]
    first_head = lambda rows: lax.broadcasted_iota(jnp.int32, (rows, LANES), 1) < HEAD_DIM
    row = lax.broadcasted_iota(jnp.int32, (2 * t, t), 0)
    col = lax.broadcasted_iota(jnp.int32, (2 * t, t), 1)
    before = col < (row & (t - 1))
    nt_dims = (((1,), (1,)), ((), ()))

    def log_one_minus_sigmoid(z):
        neg_part = jnp.minimum(z, 0.0)
        neg_relu = neg_part - z
        return neg_relu - jnp.log2(1.0 + jnp.exp2(neg_part + neg_relu))

    def suffix(lr):
        return jnp.dot(lr.astype(BF16), u, preferred_element_type=F32)

    def row_total(lr):
        return jnp.sum(lr, axis=-1, keepdims=True)

    def weighted_values(w, vals):
        w = w.astype(BF16)
        zero = jnp.zeros_like(vals)
        in_first = first_head(vals.shape[0])
        w2 = jnp.concatenate([w[:t], w[t:]], axis=1)
        v2 = jnp.concatenate([jnp.where(in_first, vals, zero), jnp.where(in_first, zero, vals)],
                             axis=0)
        return jnp.dot(w2, v2, preferred_element_type=F32)

    def score_diagonal_and_previous(i):
        here = pl.multiple_of(i * t, t)
        prev = pl.multiple_of(jnp.maximum(i - 1, 0) * t, t)
        q = q_ref[0, pl.ds(here, t), :]
        zero = jnp.zeros_like(q)
        in_first = first_head(t)
        qs = jnp.concatenate([jnp.where(in_first, q, zero), jnp.where(in_first, zero, q)], axis=0)

        keys = jnp.concatenate([k_ref[0, pl.ds(prev, t), :], k_ref[0, pl.ds(here, t), :]], axis=0)
        vals = jnp.concatenate([v_ref[0, pl.ds(prev, t), :], v_ref[0, pl.ds(here, t), :]], axis=0)
        z = lax.dot_general(qs, keys, nt_dims, preferred_element_type=F32)
        return qs, vals, z, log_one_minus_sigmoid(z)

    def weigh_diagonal_and_previous(i, qs, vals, z, lr):
        lr_p, lr_d = lr[:, :t], jnp.where(before, lr[:, t:], 0.0)
        run_p = row_total(lr_d) + jnp.where(i == 0, MASK_VALUE, 0.0)
        w_d = jnp.where(before, jnp.exp2(z[:, t:] + suffix(lr_d)), 0.0)
        w_p = jnp.exp2(z[:, :t] + suffix(lr_p) + run_p)
        acc = weighted_values(jnp.concatenate([w_p, w_d], axis=1), vals)
        run = run_p + row_total(lr_p)
        return qs, run, acc, jnp.max(run) >= EXP2_UNDERFLOW

    def earlier_tiles(i, qs, run, acc, live):
        def cond(c):
            j, live, _, _ = c
            return jnp.logical_and(j >= 0, live)

        def body(c):
            j, _, run, acc = c
            start = pl.multiple_of(j * t, t)
            z = lax.dot_general(qs, k_ref[0, pl.ds(start, t), :], nt_dims,
                                preferred_element_type=F32)
            lr = log_one_minus_sigmoid(z)
            w = jnp.exp2(z + suffix(lr) + run)
            acc = acc + weighted_values(w, v_ref[0, pl.ds(start, t), :])
            run = run + row_total(lr)
            return j - 1, jnp.max(run) >= EXP2_UNDERFLOW, run, acc

        _, _, _, acc = lax.while_loop(cond, body, (i - 2, live, run, acc))
        o_ref[0, pl.ds(pl.multiple_of(i * t, t), t), :] = acc

    def q_tiles(g, carry):
        tiles = [g * SB_UNROLL + s for s in range(SB_UNROLL)]
        heads = [weigh_diagonal_and_previous(i, *score_diagonal_and_previous(i)) for i in tiles]
        for i, head in zip(tiles, heads):
            earlier_tiles(i, *head)
        return carry

    lax.fori_loop(0, n_tiles // SB_UNROLL, q_tiles, 0)


def _sb_attention(proj, u):
    B, S, _ = proj.shape
    n_pairs = SB_HEADS * HEAD_DIM // LANES
    seq = lambda blk: pl.BlockSpec((1, S, LANES), lambda b, p: (b, 0, blk * n_pairs + p))
    return pl.pallas_call(
        _sb_kernel,
        grid=(B, n_pairs),
        in_specs=[seq(0), seq(1), seq(2), pl.BlockSpec(u.shape, lambda b, p: (0, 0))],
        out_specs=pl.BlockSpec((1, S, LANES), lambda b, p: (b, 0, p)),
        out_shape=jax.ShapeDtypeStruct((B, S, SB_HEADS * HEAD_DIM), F32),
        compiler_params=pltpu.CompilerParams(
            dimension_semantics=("parallel", "parallel"), vmem_limit_bytes=VMEM_LIMIT),
        name="sb_attn",
    )(proj, proj, proj, u)


def _swa_bias():
    w = WINDOW
    r = np.arange(2 * w)[:, None]
    c = np.arange(4 * w)[None, :]
    dist = (r % w) + w - (c % (2 * w))
    in_band = (dist >= 0) & (dist < w)
    groups = []
    for g in range(SWA_KV_HEADS):
        head = SWA_GROUP * g + 2 * (r // w) + (c // (2 * w))
        slope = np.exp2(-8.0 * (head + 1) / SWA_HEADS)
        groups.append(np.where(in_band, -slope * dist * LOG2_E, MASK_VALUE))
    rest = np.stack(groups)
    first = np.where((c % (2 * w)) >= w, rest, MASK_VALUE)
    return jnp.asarray(np.stack([first, rest]), dtype=F32)


def _swa_kernel(sink_ref, bias_ref, q_ref, kp_ref, kc_ref, vp_ref, vc_ref, *rest):
    w = WINDOW
    n_cast = (len(rest) - 1) // 2
    o_ref = rest[n_cast]
    n = pl.program_id(1)
    lane = lax.broadcasted_iota(jnp.int32, (w, LANES), 1)
    lo = lane < HEAD_DIM
    nt_dims = (((1,), (1,)), ((), ()))

    def placed(prev_ref, cur_ref):
        band = jnp.concatenate([prev_ref[0], cur_ref[0]], axis=0).astype(F32)
        swapped = pltpu.roll(band, HEAD_DIM, axis=1)
        in_lo = lax.broadcasted_iota(jnp.int32, band.shape, 1) < HEAD_DIM
        return [
            (jnp.where(in_lo, band, 0.0).astype(BF16), jnp.where(in_lo, 0.0, swapped).astype(BF16)),
            (jnp.where(in_lo, swapped, 0.0).astype(BF16), jnp.where(in_lo, 0.0, band).astype(BF16)),
        ]

    k_placed = placed(kp_ref, kc_ref)
    v_placed = placed(vp_ref, vc_ref)
    band_lo = lax.broadcasted_iota(jnp.int32, (2 * w, LANES), 1) < HEAD_DIM
    ones_lo = jnp.where(band_lo, 1.0, 0.0).astype(BF16)
    ones_hi = jnp.where(band_lo, 0.0, 1.0).astype(BF16)
    variant = jnp.minimum(n, 1)
    for a in range(SWA_TILE // w):
        rows = slice(a * w, (a + 1) * w)
        band = slice(a * w, (a + 2) * w)
        for g in range(SWA_KV_HEADS):
            cols = [slice((2 * g + rb) * LANES, (2 * g + rb + 1) * LANES) for rb in range(2)]
            q2 = jnp.concatenate([q_ref[0, rows, cols[0]], q_ref[0, rows, cols[1]]], axis=0)
            k2 = jnp.concatenate([k_placed[g][0][band], k_placed[g][1][band]], axis=0)
            v2 = jnp.concatenate([jnp.concatenate([v_placed[g][0][band], ones_lo], axis=1),
                                  jnp.concatenate([v_placed[g][1][band], ones_hi], axis=1)], axis=0)
            bias = bias_ref[variant, g] if a == 0 else bias_ref[1, g]
            s = lax.dot_general(q2, k2, nt_dims, preferred_element_type=F32) + bias
            p_rows, sink_terms = [], []
            for rb in range(2):
                p_cols, sink_cols = [], []
                for half in range(2):
                    sink = sink_ref[SWA_GROUP * g + 2 * rb + half] * LOG2_E
                    sq = s[rb * w:(rb + 1) * w, half * 2 * w:(half + 1) * 2 * w]
                    m = jnp.maximum(jnp.max(sq, axis=-1, keepdims=True), sink)
                    p_cols.append(jnp.exp2(sq - m).astype(BF16))
                    sink_cols.append(jnp.exp2(sink - m))
                p_rows.append(jnp.concatenate(p_cols, axis=1))
                sink_terms.append(jnp.where(lo, sink_cols[0], sink_cols[1]))
            o2 = jnp.dot(jnp.concatenate(p_rows, axis=0), v2, preferred_element_type=F32)
            for rb in range(2):
                sub = slice(rb * w, (rb + 1) * w)
                o_ref[0, rows, cols[rb]] = o2[sub, :LANES] / (o2[sub, LANES:] + sink_terms[rb])

    for src_ref, dst_ref in zip(rest[:n_cast], rest[n_cast + 1:]):
        dst_ref[...] = src_ref[...].astype(BF16)


def _swa_attention(proj, sinks, weights):
    B, S, _ = proj.shape
    t = SWA_TILE
    w = WINDOW
    n_steps = B * (S // t)
    q_blk = (3 * SB_HEADS * HEAD_DIM) // (SWA_HEADS * HEAD_DIM)
    k_blk = (3 * SB_HEADS + SWA_HEADS) * HEAD_DIM // LANES
    v_blk = k_blk + 1
    prev = lambda n: jnp.maximum(n * (t // w) - 1, 0)
    bias = _swa_bias()

    def row_slice(weight):
        rows = weight.shape[0] // n_steps
        assert rows * n_steps == weight.shape[0] and rows % BF16_SUBLANES == 0
        return pl.BlockSpec((rows, weight.shape[1]), lambda b, n: (b * (S // t) + n, 0))

    cast_specs = [row_slice(weight) for weight in weights]
    outs = pl.pallas_call(
        _swa_kernel,
        grid=(B, S // t),
        in_specs=[
            pl.BlockSpec(memory_space=pltpu.SMEM),
            pl.BlockSpec(bias.shape, lambda b, n: (0, 0, 0, 0)),
            pl.BlockSpec((1, t, SWA_HEADS * HEAD_DIM), lambda b, n: (b, n, q_blk)),
            pl.BlockSpec((1, w, LANES), lambda b, n: (b, prev(n), k_blk)),
            pl.BlockSpec((1, t, LANES), lambda b, n: (b, n, k_blk)),
            pl.BlockSpec((1, w, LANES), lambda b, n: (b, prev(n), v_blk)),
            pl.BlockSpec((1, t, LANES), lambda b, n: (b, n, v_blk)),
        ] + cast_specs,
        out_specs=[pl.BlockSpec((1, t, SWA_HEADS * HEAD_DIM), lambda b, n: (b, n, 0))] + cast_specs,
        out_shape=[jax.ShapeDtypeStruct((B, S, SWA_HEADS * HEAD_DIM), F32)]
        + [jax.ShapeDtypeStruct(weight.shape, BF16) for weight in weights],
        compiler_params=pltpu.CompilerParams(
            dimension_semantics=("parallel", "arbitrary"), vmem_limit_bytes=VMEM_LIMIT),
        name="swa_attn",
    )(sinks, bias, proj, proj, proj, proj, proj, *weights)
    return outs[0], outs[1:]


def _ff_chunks(d_ff):
    n_tiles = d_ff // MXU_TILE
    per = FF_CHUNK // MXU_TILE
    sizes = [per] * (n_tiles // per) + ([n_tiles % per] if n_tiles % per else [])
    return tuple(s * MXU_TILE for s in sizes)


def _post_kernel(ysb_ref, ysw_ref, x_ref, mod_ref, wo_ref, gsb_ref, gsw_ref, l1g_ref, l1b_ref,
                 wgu_ref, wd_ref, l2g_ref, l2b_ref, o_ref, *, alpha, chunks):
    sb_w = ysb_ref.shape[-1]
    d_ff = wd_ref.shape[0]
    d = x_ref.shape[-1]
    batch_row = pl.ds(pl.program_id(0), 1)
    mod = lambda k: mod_ref[batch_row, k * d:(k + 1) * d]

    def attention_tail(rows):
        n_sb = _rms_norm(ysb_ref[0, rows, :], gsb_ref[...]).astype(BF16)
        n_sw = _rms_norm(ysw_ref[0, rows, :], gsw_ref[...]).astype(BF16)
        attn = (jnp.dot(n_sb, wo_ref[:sb_w, :], preferred_element_type=F32)
                + jnp.dot(n_sw, wo_ref[sb_w:, :], preferred_element_type=F32))
        return _layer_norm(alpha * x_ref[0, rows, :] + (1.0 + mod(2)) * attn,
                           l1g_ref[...], l1b_ref[...])

    def swiglu(x1):
        h = (x1 * (1.0 + mod(4)) + mod(3)).astype(BF16)
        ffn = None
        start = 0
        for width in chunks:
            gate = jnp.dot(h, wgu_ref[:, start:start + width], preferred_element_type=F32)
            up = jnp.dot(h, wgu_ref[:, d_ff + start:d_ff + start + width],
                         preferred_element_type=F32)
            a = (gate * jax.nn.sigmoid(gate) * up).astype(BF16)
            part = jnp.dot(a, wd_ref[start:start + width, :], preferred_element_type=F32)
            ffn = part if ffn is None else ffn + part
            start += width
        return ffn

    n_rows = x_ref.shape[1]
    groups = [slice(r, r + n_rows // POST_SPLIT) for r in range(0, n_rows, n_rows // POST_SPLIT)]
    x1s = [attention_tail(rows) for rows in groups]
    ffns = [swiglu(x1) for x1 in x1s]
    for rows, x1, ffn in zip(groups, x1s, ffns):
        o_ref[0, rows, :] = _layer_norm(alpha * x1 + (1.0 + mod(5)) * ffn,
                                        l2g_ref[...], l2b_ref[...])


def _post_attention(y_sb, y_sw, x, mod, w_out, gn_sb, gn_swa, ln1_g, ln1_b, w_gu, w_down,
                    ln2_g, ln2_b, alpha):
    B, S, D = x.shape
    W = y_sb.shape[-1]
    d_ff = w_down.shape[0]
    row = lambda b, i: (b, i, 0)
    const2 = lambda b, i: (0, 0)
    resident = lambda a: pl.BlockSpec(a.shape, const2, pipeline_mode=pl.Buffered(1))
    return pl.pallas_call(
        functools.partial(_post_kernel, alpha=alpha, chunks=_ff_chunks(d_ff)),
        grid=(B, S // POST_ROW_TILE),
        in_specs=[
            pl.BlockSpec((1, POST_ROW_TILE, W), row),
            pl.BlockSpec((1, POST_ROW_TILE, W), row),
            pl.BlockSpec((1, POST_ROW_TILE, D), row),
            resident(mod),
            resident(w_out), resident(gn_sb), resident(gn_swa), resident(ln1_g), resident(ln1_b),
            resident(w_gu), resident(w_down), resident(ln2_g), resident(ln2_b),
        ],
        out_specs=pl.BlockSpec((1, POST_ROW_TILE, D), row),
        out_shape=jax.ShapeDtypeStruct((B, S, D), F32),
        compiler_params=pltpu.CompilerParams(
            dimension_semantics=("parallel", "parallel"), vmem_limit_bytes=VMEM_LIMIT),
        name="post_attn",
    )(y_sb, y_sw, x, mod, w_out, gn_sb, gn_swa, ln1_g, ln1_b, w_gu, w_down, ln2_g, ln2_b)


def kernel(x, c, w_ada, b_ada, w_in, b_in, sinks, gn_sb, gn_swa, w_out, ln1_g, ln1_b,
           w_gu, w_down, ln2_g, ln2_b):
    depth = w_ada.shape[0]
    B, S, D = x.shape
    alpha = (2.0 * depth) ** 0.25
    sb_w = SB_HEADS * HEAD_DIM
    swa_qw = SWA_HEADS * HEAD_DIM
    d_in = w_in.shape[-1]
    cols = np.arange(d_in)
    is_q = (cols < sb_w) | ((cols >= 3 * sb_w) & (cols < 3 * sb_w + swa_qw))
    col_scale = jnp.asarray(np.where(is_q, LOG2_E / math.sqrt(HEAD_DIM), 1.0)[None, :], dtype=F32)
    tri = jnp.asarray(np.arange(SB_TILE)[:, None] >= np.arange(SB_TILE)[None, :], dtype=BF16)
    row2 = lambda a: a.reshape(1, -1)

    for l in range(depth):
        mod = _mod(c, w_ada[l], row2(b_ada[l]))
        proj = _in_proj(x, mod, w_in[l], row2(b_in[l]), col_scale)
        y_sb = _sb_attention(proj, tri)
        y_sw, (w_o, w_g, w_d) = _swa_attention(proj, sinks[l], [w_out[l], w_gu[l], w_down[l]])
        x = _post_attention(y_sb, y_sw, x, mod, w_o, row2(gn_sb[l]), row2(gn_swa[l]),
                            row2(ln1_g[l]), row2(ln1_b[l]), w_g, w_d, row2(ln2_g[l]),
                            row2(ln2_b[l]), alpha)
    return x
```

```python
import functools
import math

import jax
import jax.numpy as jnp
import numpy as np
from jax import lax
from jax.experimental import pallas as pl
from jax.experimental.pallas import tpu as pltpu

F32 = jnp.float32
BF16 = jnp.bfloat16

HEAD_DIM = 64
SB_HEADS = 8
SWA_HEADS = 8
SWA_KV_HEADS = 2
SWA_GROUP = SWA_HEADS // SWA_KV_HEADS
WINDOW = 128
LN_EPS = 1e-5
RMS_EPS = 1e-6
MASK_VALUE = -1e30
EXP2_UNDERFLOW = -150.0
LOG2_E = 1.4426950408889634
LANES = 128
BF16_SUBLANES = 16

SB_TILE = 256
SB_UNROLL = 8
SWA_TILE = 1024
ROW_TILE = 1024
POST_ROW_TILE = 1024
POST_SPLIT = 4
MXU_TILE = 256
FF_CHUNK = 768
MOD_COLS = 1536
VMEM_LIMIT = 60 * 1024 * 1024


def _layer_norm(r, g, b):
    mu = jnp.mean(r, axis=-1, keepdims=True)
    d = r - mu
    var = jnp.mean(d * d, axis=-1, keepdims=True)
    return d * lax.rsqrt(var + LN_EPS) * g + b


def _rms_norm(y, g):
    ms = jnp.mean(y * y, axis=-1, keepdims=True)
    return y * lax.rsqrt(ms + RMS_EPS) * g


def _mod_kernel(ct_ref, wa_ref, wb_ref, b_ref, o_ref):
    ct = ct_ref[...]
    s = ct * jax.nn.sigmoid(ct)
    half = wa_ref.shape[0]
    for b in range(ct.shape[1]):
        col = s[:, b:b + 1]
        o_ref[b:b + 1, :] = (jnp.sum(wa_ref[...] * col[:half], axis=0, keepdims=True)
                             + jnp.sum(wb_ref[...] * col[half:], axis=0, keepdims=True)
                             + b_ref[...])


def _mod(c, w_ada, b_ada):
    B, D = c.shape
    N = w_ada.shape[1]
    return pl.pallas_call(
        _mod_kernel,
        grid=(N // MOD_COLS,),
        in_specs=[
            pl.BlockSpec((D, B), lambda n: (0, 0)),
            pl.BlockSpec((D // 2, MOD_COLS), lambda n: (0, n)),
            pl.BlockSpec((D // 2, MOD_COLS), lambda n: (1, n)),
            pl.BlockSpec((1, MOD_COLS), lambda n: (0, n)),
        ],
        out_specs=pl.BlockSpec((B, MOD_COLS), lambda n: (0, n)),
        out_shape=jax.ShapeDtypeStruct((B, N), F32),
        compiler_params=pltpu.CompilerParams(
            dimension_semantics=("arbitrary",), vmem_limit_bytes=VMEM_LIMIT),
        name="mod",
    )(c.T, w_ada, w_ada, b_ada)


def _inproj_kernel(x_ref, mod_ref, w_ref, b_ref, cs_ref, o_ref, wb_ref, *, col_chunk):
    @pl.when(jnp.logical_and(pl.program_id(0) == 0, pl.program_id(1) == 0))
    def _():
        wb_ref[...] = w_ref[...].astype(BF16)

    d = x_ref.shape[-1]
    batch_row = pl.ds(pl.program_id(0), 1)
    sh = mod_ref[batch_row, 0:d]
    sc = mod_ref[batch_row, d:2 * d]
    h = (x_ref[0] * (1.0 + sc) + sh).astype(BF16)
    n_cols = w_ref.shape[1]
    for n in range(n_cols // col_chunk):
        sl = slice(n * col_chunk, (n + 1) * col_chunk)
        p = jnp.dot(h, wb_ref[:, sl], preferred_element_type=F32)
        o_ref[0, :, sl] = ((p + b_ref[:, sl]) * cs_ref[:, sl]).astype(BF16)


def _in_proj(x, mod, w_in, b_in, col_scale):
    B, S, D = x.shape
    N = w_in.shape[1]
    return pl.pallas_call(
        functools.partial(_inproj_kernel, col_chunk=768),
        grid=(B, S // ROW_TILE),
        in_specs=[
            pl.BlockSpec((1, ROW_TILE, D), lambda b, i: (b, i, 0)),
            pl.BlockSpec(mod.shape, lambda b, i: (0, 0)),
            pl.BlockSpec((D, N), lambda b, i: (0, 0), pipeline_mode=pl.Buffered(1)),
            pl.BlockSpec((1, N), lambda b, i: (0, 0)),
            pl.BlockSpec((1, N), lambda b, i: (0, 0)),
        ],
        out_specs=pl.BlockSpec((1, ROW_TILE, N), lambda b, i: (b, i, 0)),
        out_shape=jax.ShapeDtypeStruct((B, S, N), BF16),
        scratch_shapes=[pltpu.VMEM((D, N), BF16)],
        compiler_params=pltpu.CompilerParams(
            dimension_semantics=("arbitrary", "arbitrary"), vmem_limit_bytes=VMEM_LIMIT),
        name="in_proj",
    )(x, mod, w_in, b_in, col_scale)


def _sb_kernel(q_ref, k_ref, v_ref, u_ref, o_ref):
    t = SB_TILE
    n_tiles = q_ref.shape[1] // t
    u = u_ref[...]
    first_head = lambda rows: lax.broadcasted_iota(jnp.int32, (rows, LANES), 1) < HEAD_DIM
    row = lax.broadcasted_iota(jnp.int32, (2 * t, t), 0)
    col = lax.broadcasted_iota(jnp.int32, (2 * t, t), 1)
    before = col < (row & (t - 1))
    nt_dims = (((1,), (1,)), ((), ()))

    def log_one_minus_sigmoid(z):
        neg_part = jnp.minimum(z, 0.0)
        neg_relu = neg_part - z
        return neg_relu - jnp.log2(1.0 + jnp.exp2(neg_part + neg_relu))

    def suffix(lr):
        return jnp.dot(lr.astype(BF16), u, preferred_element_type=F32)

    def row_total(lr):
        return jnp.sum(lr, axis=-1, keepdims=True)

    def weighted_values(w, vals):
        w = w.astype(BF16)
        zero = jnp.zeros_like(vals)
        in_first = first_head(vals.shape[0])
        w2 = jnp.concatenate([w[:t], w[t:]], axis=1)
        v2 = jnp.concatenate([jnp.where(in_first, vals, zero), jnp.where(in_first, zero, vals)],
                             axis=0)
        return jnp.dot(w2, v2, preferred_element_type=F32)

    def score_diagonal_and_previous(i):
        here = pl.multiple_of(i * t, t)
        prev = pl.multiple_of(jnp.maximum(i - 1, 0) * t, t)
        q = q_ref[0, pl.ds(here, t), :]
        zero = jnp.zeros_like(q)
        in_first = first_head(t)
        qs = jnp.concatenate([jnp.where(in_first, q, zero), jnp.where(in_first, zero, q)], axis=0)

        keys = jnp.concatenate([k_ref[0, pl.ds(prev, t), :], k_ref[0, pl.ds(here, t), :]], axis=0)
        vals = jnp.concatenate([v_ref[0, pl.ds(prev, t), :], v_ref[0, pl.ds(here, t), :]], axis=0)
        z = lax.dot_general(qs, keys, nt_dims, preferred_element_type=F32)
        return qs, vals, z, log_one_minus_sigmoid(z)

    def weigh_diagonal_and_previous(i, qs, vals, z, lr):
        lr_p, lr_d = lr[:, :t], jnp.where(before, lr[:, t:], 0.0)
        run_p = row_total(lr_d) + jnp.where(i == 0, MASK_VALUE, 0.0)
        w_d = jnp.where(before, jnp.exp2(z[:, t:] + suffix(lr_d)), 0.0)
        w_p = jnp.exp2(z[:, :t] + suffix(lr_p) + run_p)
        acc = weighted_values(jnp.concatenate([w_p, w_d], axis=1), vals)
        run = run_p + row_total(lr_p)
        return qs, run, acc, jnp.max(run) >= EXP2_UNDERFLOW

    def earlier_tiles(i, qs, run, acc, live):
        def cond(c):
            j, live, _, _ = c
            return jnp.logical_and(j >= 0, live)

        def body(c):
            j, _, run, acc = c
            start = pl.multiple_of(j * t, t)
            z = lax.dot_general(qs, k_ref[0, pl.ds(start, t), :], nt_dims,
                                preferred_element_type=F32)
            lr = log_one_minus_sigmoid(z)
            w = jnp.exp2(z + suffix(lr) + run)
            acc = acc + weighted_values(w, v_ref[0, pl.ds(start, t), :])
            run = run + row_total(lr)
            return j - 1, jnp.max(run) >= EXP2_UNDERFLOW, run, acc

        _, _, _, acc = lax.while_loop(cond, body, (i - 2, live, run, acc))
        o_ref[0, pl.ds(pl.multiple_of(i * t, t), t), :] = acc

    def q_tiles(g, carry):
        tiles = [g * SB_UNROLL + s for s in range(SB_UNROLL)]
        heads = [weigh_diagonal_and_previous(i, *score_diagonal_and_previous(i)) for i in tiles]
        for i, head in zip(tiles, heads):
            earlier_tiles(i, *head)
        return carry

    lax.fori_loop(0, n_tiles // SB_UNROLL, q_tiles, 0)


def _sb_attention(proj, u):
    B, S, _ = proj.shape
    n_pairs = SB_HEADS * HEAD_DIM // LANES
    seq = lambda blk: pl.BlockSpec((1, S, LANES), lambda b, p: (b, 0, blk * n_pairs + p))
    return pl.pallas_call(
        _sb_kernel,
        grid=(B, n_pairs),
        in_specs=[seq(0), seq(1), seq(2), pl.BlockSpec(u.shape, lambda b, p: (0, 0))],
        out_specs=pl.BlockSpec((1, S, LANES), lambda b, p: (b, 0, p)),
        out_shape=jax.ShapeDtypeStruct((B, S, SB_HEADS * HEAD_DIM), F32),
        compiler_params=pltpu.CompilerParams(
            dimension_semantics=("parallel", "parallel"), vmem_limit_bytes=VMEM_LIMIT),
        name="sb_attn",
    )(proj, proj, proj, u)


def _swa_bias():
    w = WINDOW
    r = np.arange(2 * w)[:, None]
    c = np.arange(4 * w)[None, :]
    dist = (r % w) + w - (c % (2 * w))
    in_band = (dist >= 0) & (dist < w)
    groups = []
    for g in range(SWA_KV_HEADS):
        head = SWA_GROUP * g + 2 * (r // w) + (c // (2 * w))
        slope = np.exp2(-8.0 * (head + 1) / SWA_HEADS)
        groups.append(np.where(in_band, -slope * dist * LOG2_E, MASK_VALUE))
    rest = np.stack(groups)
    first = np.where((c % (2 * w)) >= w, rest, MASK_VALUE)
    return jnp.asarray(np.stack([first, rest]), dtype=F32)


def _swa_kernel(sink_ref, bias_ref, q_ref, kp_ref, kc_ref, vp_ref, vc_ref, *rest):
    w = WINDOW
    n_cast = (len(rest) - 1) // 2
    o_ref = rest[n_cast]
    n = pl.program_id(1)
    lane = lax.broadcasted_iota(jnp.int32, (w, LANES), 1)
    lo = lane < HEAD_DIM
    nt_dims = (((1,), (1,)), ((), ()))

    def placed(prev_ref, cur_ref):
        band = jnp.concatenate([prev_ref[0], cur_ref[0]], axis=0).astype(F32)
        swapped = pltpu.roll(band, HEAD_DIM, axis=1)
        in_lo = lax.broadcasted_iota(jnp.int32, band.shape, 1) < HEAD_DIM
        return [
            (jnp.where(in_lo, band, 0.0).astype(BF16), jnp.where(in_lo, 0.0, swapped).astype(BF16)),
            (jnp.where(in_lo, swapped, 0.0).astype(BF16), jnp.where(in_lo, 0.0, band).astype(BF16)),
        ]

    k_placed = placed(kp_ref, kc_ref)
    v_placed = placed(vp_ref, vc_ref)
    band_lo = lax.broadcasted_iota(jnp.int32, (2 * w, LANES), 1) < HEAD_DIM
    ones_lo = jnp.where(band_lo, 1.0, 0.0).astype(BF16)
    ones_hi = jnp.where(band_lo, 0.0, 1.0).astype(BF16)
    variant = jnp.minimum(n, 1)
    for a in range(SWA_TILE // w):
        rows = slice(a * w, (a + 1) * w)
        band = slice(a * w, (a + 2) * w)
        for g in range(SWA_KV_HEADS):
            cols = [slice((2 * g + rb) * LANES, (2 * g + rb + 1) * LANES) for rb in range(2)]
            q2 = jnp.concatenate([q_ref[0, rows, cols[0]], q_ref[0, rows, cols[1]]], axis=0)
            k2 = jnp.concatenate([k_placed[g][0][band], k_placed[g][1][band]], axis=0)
            v2 = jnp.concatenate([jnp.concatenate([v_placed[g][0][band], ones_lo], axis=1),
                                  jnp.concatenate([v_placed[g][1][band], ones_hi], axis=1)], axis=0)
            bias = bias_ref[variant, g] if a == 0 else bias_ref[1, g]
            s = lax.dot_general(q2, k2, nt_dims, preferred_element_type=F32) + bias
            p_rows, sink_terms = [], []
            for rb in range(2):
                p_cols, sink_cols = [], []
                for half in range(2):
                    sink = sink_ref[SWA_GROUP * g + 2 * rb + half] * LOG2_E
                    sq = s[rb * w:(rb + 1) * w, half * 2 * w:(half + 1) * 2 * w]
                    m = jnp.maximum(jnp.max(sq, axis=-1, keepdims=True), sink)
                    p_cols.append(jnp.exp2(sq - m).astype(BF16))
                    sink_cols.append(jnp.exp2(sink - m))
                p_rows.append(jnp.concatenate(p_cols, axis=1))
                sink_terms.append(jnp.where(lo, sink_cols[0], sink_cols[1]))
            o2 = jnp.dot(jnp.concatenate(p_rows, axis=0), v2, preferred_element_type=F32)
            for rb in range(2):
                sub = slice(rb * w, (rb + 1) * w)
                o_ref[0, rows, cols[rb]] = o2[sub, :LANES] / (o2[sub, LANES:] + sink_terms[rb])

    for src_ref, dst_ref in zip(rest[:n_cast], rest[n_cast + 1:]):
        dst_ref[...] = src_ref[...].astype(BF16)


def _swa_attention(proj, sinks, weights):
    B, S, _ = proj.shape
    t = SWA_TILE
    w = WINDOW
    n_steps = B * (S // t)
    q_blk = (3 * SB_HEADS * HEAD_DIM) // (SWA_HEADS * HEAD_DIM)
    k_blk = (3 * SB_HEADS + SWA_HEADS) * HEAD_DIM // LANES
    v_blk = k_blk + 1
    prev = lambda n: jnp.maximum(n * (t // w) - 1, 0)
    bias = _swa_bias()

    def row_slice(weight):
        rows = weight.shape[0] // n_steps
        assert rows * n_steps == weight.shape[0] and rows % BF16_SUBLANES == 0
        return pl.BlockSpec((rows, weight.shape[1]), lambda b, n: (b * (S // t) + n, 0))

    cast_specs = [row_slice(weight) for weight in weights]
    outs = pl.pallas_call(
        _swa_kernel,
        grid=(B, S // t),
        in_specs=[
            pl.BlockSpec(memory_space=pltpu.SMEM),
            pl.BlockSpec(bias.shape, lambda b, n: (0, 0, 0, 0)),
            pl.BlockSpec((1, t, SWA_HEADS * HEAD_DIM), lambda b, n: (b, n, q_blk)),
            pl.BlockSpec((1, w, LANES), lambda b, n: (b, prev(n), k_blk)),
            pl.BlockSpec((1, t, LANES), lambda b, n: (b, n, k_blk)),
            pl.BlockSpec((1, w, LANES), lambda b, n: (b, prev(n), v_blk)),
            pl.BlockSpec((1, t, LANES), lambda b, n: (b, n, v_blk)),
        ] + cast_specs,
        out_specs=[pl.BlockSpec((1, t, SWA_HEADS * HEAD_DIM), lambda b, n: (b, n, 0))] + cast_specs,
        out_shape=[jax.ShapeDtypeStruct((B, S, SWA_HEADS * HEAD_DIM), F32)]
        + [jax.ShapeDtypeStruct(weight.shape, BF16) for weight in weights],
        compiler_params=pltpu.CompilerParams(
            dimension_semantics=("parallel", "arbitrary"), vmem_limit_bytes=VMEM_LIMIT),
        name="swa_attn",
    )(sinks, bias, proj, proj, proj, proj, proj, *weights)
    return outs[0], outs[1:]


def _ff_chunks(d_ff):
    n_tiles = d_ff // MXU_TILE
    per = FF_CHUNK // MXU_TILE
    sizes = [per] * (n_tiles // per) + ([n_tiles % per] if n_tiles % per else [])
    return tuple(s * MXU_TILE for s in sizes)


def _post_kernel(ysb_ref, ysw_ref, x_ref, mod_ref, wo_ref, gsb_ref, gsw_ref, l1g_ref, l1b_ref,
                 wgu_ref, wd_ref, l2g_ref, l2b_ref, o_ref, *, alpha, chunks):
    sb_w = ysb_ref.shape[-1]
    d_ff = wd_ref.shape[0]
    d = x_ref.shape[-1]
    batch_row = pl.ds(pl.program_id(0), 1)
    mod = lambda k: mod_ref[batch_row, k * d:(k + 1) * d]

    def attention_tail(rows):
        n_sb = _rms_norm(ysb_ref[0, rows, :], gsb_ref[...]).astype(BF16)
        n_sw = _rms_norm(ysw_ref[0, rows, :], gsw_ref[...]).astype(BF16)
        attn = (jnp.dot(n_sb, wo_ref[:sb_w, :], preferred_element_type=F32)
                + jnp.dot(n_sw, wo_ref[sb_w:, :], preferred_element_type=F32))
        return _layer_norm(alpha * x_ref[0, rows, :] + (1.0 + mod(2)) * attn,
                           l1g_ref[...], l1b_ref[...])

    def swiglu(x1):
        h = (x1 * (1.0 + mod(4)) + mod(3)).astype(BF16)
        ffn = None
        start = 0
        for width in chunks:
            gate = jnp.dot(h, wgu_ref[:, start:start + width], preferred_element_type=F32)
            up = jnp.dot(h, wgu_ref[:, d_ff + start:d_ff + start + width],
                         preferred_element_type=F32)
            a = (gate * jax.nn.sigmoid(gate) * up).astype(BF16)
            part = jnp.dot(a, wd_ref[start:start + width, :], preferred_element_type=F32)
            ffn = part if ffn is None else ffn + part
            start += width
        return ffn

    n_rows = x_ref.shape[1]
    groups = [slice(r, r + n_rows // POST_SPLIT) for r in range(0, n_rows, n_rows // POST_SPLIT)]
    x1s = [attention_tail(rows) for rows in groups]
    ffns = [swiglu(x1) for x1 in x1s]
    for rows, x1, ffn in zip(groups, x1s, ffns):
        o_ref[0, rows, :] = _layer_norm(alpha * x1 + (1.0 + mod(5)) * ffn,
                                        l2g_ref[...], l2b_ref[...])


def _post_attention(y_sb, y_sw, x, mod, w_out, gn_sb, gn_swa, ln1_g, ln1_b, w_gu, w_down,
                    ln2_g, ln2_b, alpha):
    B, S, D = x.shape
    W = y_sb.shape[-1]
    d_ff = w_down.shape[0]
    row = lambda b, i: (b, i, 0)
    const2 = lambda b, i: (0, 0)
    resident = lambda a: pl.BlockSpec(a.shape, const2, pipeline_mode=pl.Buffered(1))
    return pl.pallas_call(
        functools.partial(_post_kernel, alpha=alpha, chunks=_ff_chunks(d_ff)),
        grid=(B, S // POST_ROW_TILE),
        in_specs=[
            pl.BlockSpec((1, POST_ROW_TILE, W), row),
            pl.BlockSpec((1, POST_ROW_TILE, W), row),
            pl.BlockSpec((1, POST_ROW_TILE, D), row),
            resident(mod),
            resident(w_out), resident(gn_sb), resident(gn_swa), resident(ln1_g), resident(ln1_b),
            resident(w_gu), resident(w_down), resident(ln2_g), resident(ln2_b),
        ],
        out_specs=pl.BlockSpec((1, POST_ROW_TILE, D), row),
        out_shape=jax.ShapeDtypeStruct((B, S, D), F32),
        compiler_params=pltpu.CompilerParams(
            dimension_semantics=("parallel", "parallel"), vmem_limit_bytes=VMEM_LIMIT),
        name="post_attn",
    )(y_sb, y_sw, x, mod, w_out, gn_sb, gn_swa, ln1_g, ln1_b, w_gu, w_down, ln2_g, ln2_b)


def kernel(x, c, w_ada, b_ada, w_in, b_in, sinks, gn_sb, gn_swa, w_out, ln1_g, ln1_b,
           w_gu, w_down, ln2_g, ln2_b):
    depth = w_ada.shape[0]
    B, S, D = x.shape
    alpha = (2.0 * depth) ** 0.25
    sb_w = SB_HEADS * HEAD_DIM
    swa_qw = SWA_HEADS * HEAD_DIM
    d_in = w_in.shape[-1]
    cols = np.arange(d_in)
    is_q = (cols < sb_w) | ((cols >= 3 * sb_w) & (cols < 3 * sb_w + swa_qw))
    col_scale = jnp.asarray(np.where(is_q, LOG2_E / math.sqrt(HEAD_DIM), 1.0)[None, :], dtype=F32)
    tri = jnp.asarray(np.arange(SB_TILE)[:, None] >= np.arange(SB_TILE)[None, :], dtype=BF16)
    row2 = lambda a: a.reshape(1, -1)

    for l in range(depth):
        mod = _mod(c, w_ada[l], row2(b_ada[l]))
        proj = _in_proj(x, mod, w_in[l], row2(b_in[l]), col_scale)
        y_sb = _sb_attention(proj, tri)
        y_sw, (w_o, w_g, w_d) = _swa_attention(proj, sinks[l], [w_out[l], w_gu[l], w_down[l]])
        x = _post_attention(y_sb, y_sw, x, mod, w_o, row2(gn_sb[l]), row2(gn_swa[l]),
                            row2(ln1_g[l]), row2(ln1_b[l]), w_g, w_d, row2(ln2_g[l]),
                            row2(ln2_b[l]), alpha)
    return x
```

```python
import functools
import math

import jax
import jax.numpy as jnp
import numpy as np
from jax import lax
from jax.experimental import pallas as pl
from jax.experimental.pallas import tpu as pltpu

F32 = jnp.float32
BF16 = jnp.bfloat16

HEAD_DIM = 64
SB_HEADS = 8
SWA_HEADS = 8
SWA_KV_HEADS = 2
SWA_GROUP = SWA_HEADS // SWA_KV_HEADS
WINDOW = 128
LN_EPS = 1e-5
RMS_EPS = 1e-6
MASK_VALUE = -1e30
EXP2_UNDERFLOW = -150.0
LOG2_E = 1.4426950408889634
LANES = 128
BF16_SUBLANES = 16

SB_TILE = 256
SB_UNROLL = 8
SWA_TILE = 1024
SWA_LAG = 1
ROW_TILE = 1024
POST_ROW_TILE = 1024
POST_SPLIT = 4
MXU_TILE = 256
FF_CHUNK = 768
MOD_COLS = 1536
VMEM_LIMIT = 60 * 1024 * 1024


def _layer_norm(r, g, b):
    mu = jnp.mean(r, axis=-1, keepdims=True)
    d = r - mu
    var = jnp.mean(d * d, axis=-1, keepdims=True)
    return d * lax.rsqrt(var + LN_EPS) * g + b


def _rms_norm(y, g):
    ms = jnp.mean(y * y, axis=-1, keepdims=True)
    return y * lax.rsqrt(ms + RMS_EPS) * g


def _mod_kernel(ct_ref, wa_ref, wb_ref, b_ref, o_ref):
    ct = ct_ref[...]
    s = ct * jax.nn.sigmoid(ct)
    half = wa_ref.shape[0]
    for b in range(ct.shape[1]):
        col = s[:, b:b + 1]
        o_ref[b:b + 1, :] = (jnp.sum(wa_ref[...] * col[:half], axis=0, keepdims=True)
                             + jnp.sum(wb_ref[...] * col[half:], axis=0, keepdims=True)
                             + b_ref[...])


def _mod(c, w_ada, b_ada):
    B, D = c.shape
    N = w_ada.shape[1]
    return pl.pallas_call(
        _mod_kernel,
        grid=(N // MOD_COLS,),
        in_specs=[
            pl.BlockSpec((D, B), lambda n: (0, 0)),
            pl.BlockSpec((D // 2, MOD_COLS), lambda n: (0, n)),
            pl.BlockSpec((D // 2, MOD_COLS), lambda n: (1, n)),
            pl.BlockSpec((1, MOD_COLS), lambda n: (0, n)),
        ],
        out_specs=pl.BlockSpec((B, MOD_COLS), lambda n: (0, n)),
        out_shape=jax.ShapeDtypeStruct((B, N), F32),
        compiler_params=pltpu.CompilerParams(
            dimension_semantics=("arbitrary",), vmem_limit_bytes=VMEM_LIMIT),
        name="mod",
    )(c.T, w_ada, w_ada, b_ada)


def _inproj_kernel(x_ref, mod_ref, w_ref, b_ref, cs_ref, o_ref, wb_ref, *, col_chunk):
    @pl.when(jnp.logical_and(pl.program_id(0) == 0, pl.program_id(1) == 0))
    def _():
        wb_ref[...] = w_ref[...].astype(BF16)

    d = x_ref.shape[-1]
    batch_row = pl.ds(pl.program_id(0), 1)
    sh = mod_ref[batch_row, 0:d]
    sc = mod_ref[batch_row, d:2 * d]
    h = (x_ref[0] * (1.0 + sc) + sh).astype(BF16)
    n_cols = w_ref.shape[1]
    for n in range(n_cols // col_chunk):
        sl = slice(n * col_chunk, (n + 1) * col_chunk)
        p = jnp.dot(h, wb_ref[:, sl], preferred_element_type=F32)
        o_ref[0, :, sl] = ((p + b_ref[:, sl]) * cs_ref[:, sl]).astype(BF16)


def _in_proj(x, mod, w_in, b_in, col_scale):
    B, S, D = x.shape
    N = w_in.shape[1]
    return pl.pallas_call(
        functools.partial(_inproj_kernel, col_chunk=768),
        grid=(B, S // ROW_TILE),
        in_specs=[
            pl.BlockSpec((1, ROW_TILE, D), lambda b, i: (b, i, 0)),
            pl.BlockSpec(mod.shape, lambda b, i: (0, 0)),
            pl.BlockSpec((D, N), lambda b, i: (0, 0), pipeline_mode=pl.Buffered(1)),
            pl.BlockSpec((1, N), lambda b, i: (0, 0)),
            pl.BlockSpec((1, N), lambda b, i: (0, 0)),
        ],
        out_specs=pl.BlockSpec((1, ROW_TILE, N), lambda b, i: (b, i, 0)),
        out_shape=jax.ShapeDtypeStruct((B, S, N), BF16),
        scratch_shapes=[pltpu.VMEM((D, N), BF16)],
        compiler_params=pltpu.CompilerParams(
            dimension_semantics=("arbitrary", "arbitrary"), vmem_limit_bytes=VMEM_LIMIT),
        name="in_proj",
    )(x, mod, w_in, b_in, col_scale)


def _sb_kernel(q_ref, k_ref, v_ref, u_ref, o_ref):
    t = SB_TILE
    n_tiles = q_ref.shape[1] // t
    u = u_ref[...]
    first_head = lambda rows: lax.broadcasted_iota(jnp.int32, (rows, LANES), 1) < HEAD_DIM
    row = lax.broadcasted_iota(jnp.int32, (2 * t, t), 0)
    col = lax.broadcasted_iota(jnp.int32, (2 * t, t), 1)
    before = col < (row & (t - 1))
    nt_dims = (((1,), (1,)), ((), ()))

    def log_one_minus_sigmoid(z):
        neg_part = jnp.minimum(z, 0.0)
        neg_relu = neg_part - z
        return neg_relu - jnp.log2(1.0 + jnp.exp2(neg_part + neg_relu))

    def suffix(lr):
        return jnp.dot(lr.astype(BF16), u, preferred_element_type=F32)

    def row_total(lr):
        return jnp.sum(lr, axis=-1, keepdims=True)

    def weighted_values(w, vals):
        w = w.astype(BF16)
        zero = jnp.zeros_like(vals)
        in_first = first_head(vals.shape[0])
        w2 = jnp.concatenate([w[:t], w[t:]], axis=1)
        v2 = jnp.concatenate([jnp.where(in_first, vals, zero), jnp.where(in_first, zero, vals)],
                             axis=0)
        return jnp.dot(w2, v2, preferred_element_type=F32)

    def score_diagonal_and_previous(i):
        here = pl.multiple_of(i * t, t)
        prev = pl.multiple_of(jnp.maximum(i - 1, 0) * t, t)
        q = q_ref[0, pl.ds(here, t), :]
        zero = jnp.zeros_like(q)
        in_first = first_head(t)
        qs = jnp.concatenate([jnp.where(in_first, q, zero), jnp.where(in_first, zero, q)], axis=0)

        keys = jnp.concatenate([k_ref[0, pl.ds(prev, t), :], k_ref[0, pl.ds(here, t), :]], axis=0)
        vals = jnp.concatenate([v_ref[0, pl.ds(prev, t), :], v_ref[0, pl.ds(here, t), :]], axis=0)
        z = lax.dot_general(qs, keys, nt_dims, preferred_element_type=F32)
        return qs, vals, z, log_one_minus_sigmoid(z)

    def weigh_diagonal_and_previous(i, qs, vals, z, lr):
        lr_p, lr_d = lr[:, :t], jnp.where(before, lr[:, t:], 0.0)
        run_p = row_total(lr_d) + jnp.where(i == 0, MASK_VALUE, 0.0)
        w_d = jnp.where(before, jnp.exp2(z[:, t:] + suffix(lr_d)), 0.0)
        w_p = jnp.exp2(z[:, :t] + suffix(lr_p) + run_p)
        acc = weighted_values(jnp.concatenate([w_p, w_d], axis=1), vals)
        run = run_p + row_total(lr_p)
        return qs, run, acc, jnp.max(run) >= EXP2_UNDERFLOW

    def earlier_tiles(i, qs, run, acc, live):
        def cond(c):
            j, live, _, _ = c
            return jnp.logical_and(j >= 0, live)

        def body(c):
            j, _, run, acc = c
            start = pl.multiple_of(j * t, t)
            z = lax.dot_general(qs, k_ref[0, pl.ds(start, t), :], nt_dims,
                                preferred_element_type=F32)
            lr = log_one_minus_sigmoid(z)
            w = jnp.exp2(z + suffix(lr) + run)
            acc = acc + weighted_values(w, v_ref[0, pl.ds(start, t), :])
            run = run + row_total(lr)
            return j - 1, jnp.max(run) >= EXP2_UNDERFLOW, run, acc

        _, _, _, acc = lax.while_loop(cond, body, (i - 2, live, run, acc))
        o_ref[0, pl.ds(pl.multiple_of(i * t, t), t), :] = acc

    def q_tiles(g, carry):
        tiles = [g * SB_UNROLL + s for s in range(SB_UNROLL)]
        heads = [weigh_diagonal_and_previous(i, *score_diagonal_and_previous(i)) for i in tiles]
        for i, head in zip(tiles, heads):
            earlier_tiles(i, *head)
        return carry

    lax.fori_loop(0, n_tiles // SB_UNROLL, q_tiles, 0)


def _sb_attention(proj, u):
    B, S, _ = proj.shape
    n_pairs = SB_HEADS * HEAD_DIM // LANES
    seq = lambda blk: pl.BlockSpec((1, S, LANES), lambda b, p: (b, 0, blk * n_pairs + p))
    return pl.pallas_call(
        _sb_kernel,
        grid=(B, n_pairs),
        in_specs=[seq(0), seq(1), seq(2), pl.BlockSpec(u.shape, lambda b, p: (0, 0))],
        out_specs=pl.BlockSpec((1, S, LANES), lambda b, p: (b, 0, p)),
        out_shape=jax.ShapeDtypeStruct((B, S, SB_HEADS * HEAD_DIM), F32),
        compiler_params=pltpu.CompilerParams(
            dimension_semantics=("parallel", "parallel"), vmem_limit_bytes=VMEM_LIMIT),
        name="sb_attn",
    )(proj, proj, proj, u)


def _swa_bias():
    w = WINDOW
    r = np.arange(2 * w)[:, None]
    c = np.arange(4 * w)[None, :]
    dist = (r % w) + w - (c % (2 * w))
    in_band = (dist >= 0) & (dist < w)
    groups = []
    for g in range(SWA_KV_HEADS):
        head = SWA_GROUP * g + 2 * (r // w) + (c // (2 * w))
        slope = np.exp2(-8.0 * (head + 1) / SWA_HEADS)
        groups.append(np.where(in_band, -slope * dist * LOG2_E, MASK_VALUE))
    rest = np.stack(groups)
    first = np.where((c % (2 * w)) >= w, rest, MASK_VALUE)
    return jnp.asarray(np.stack([first, rest]), dtype=F32)


def _swa_kernel(sink_ref, bias_ref, q_ref, kp_ref, kc_ref, vp_ref, vc_ref, *rest):
    w = WINDOW
    n_cast = (len(rest) - 1) // 2
    o_ref = rest[n_cast]
    n = pl.program_id(1)
    lane = lax.broadcasted_iota(jnp.int32, (w, LANES), 1)
    lo = lane < HEAD_DIM
    nt_dims = (((1,), (1,)), ((), ()))

    def placed(prev_ref, cur_ref):
        band = jnp.concatenate([prev_ref[0], cur_ref[0]], axis=0).astype(F32)
        swapped = pltpu.roll(band, HEAD_DIM, axis=1)
        in_lo = lax.broadcasted_iota(jnp.int32, band.shape, 1) < HEAD_DIM
        return [
            (jnp.where(in_lo, band, 0.0).astype(BF16), jnp.where(in_lo, 0.0, swapped).astype(BF16)),
            (jnp.where(in_lo, swapped, 0.0).astype(BF16), jnp.where(in_lo, 0.0, band).astype(BF16)),
        ]

    k_placed = placed(kp_ref, kc_ref)
    v_placed = placed(vp_ref, vc_ref)
    band_lo = lax.broadcasted_iota(jnp.int32, (2 * w, LANES), 1) < HEAD_DIM
    ones_lo = jnp.where(band_lo, 1.0, 0.0).astype(BF16)
    ones_hi = jnp.where(band_lo, 0.0, 1.0).astype(BF16)
    variant = jnp.minimum(n, 1)

    def scores(a, g):
        rows = slice(a * w, (a + 1) * w)
        band = slice(a * w, (a + 2) * w)
        cols = [slice((2 * g + rb) * LANES, (2 * g + rb + 1) * LANES) for rb in range(2)]
        q2 = jnp.concatenate([q_ref[0, rows, cols[0]], q_ref[0, rows, cols[1]]], axis=0)
        k2 = jnp.concatenate([k_placed[g][0][band], k_placed[g][1][band]], axis=0)
        bias = bias_ref[variant, g] if a == 0 else bias_ref[1, g]
        return lax.dot_general(q2, k2, nt_dims, preferred_element_type=F32) + bias

    def softmax_numerators(g, s):
        p_rows, sink_terms = [], []
        for rb in range(2):
            p_cols, sink_cols = [], []
            for half in range(2):
                sink = sink_ref[SWA_GROUP * g + 2 * rb + half] * LOG2_E
                sq = s[rb * w:(rb + 1) * w, half * 2 * w:(half + 1) * 2 * w]
                m = jnp.maximum(jnp.max(sq, axis=-1, keepdims=True), sink)
                p_cols.append(jnp.exp2(sq - m).astype(BF16))
                sink_cols.append(jnp.exp2(sink - m))
            p_rows.append(jnp.concatenate(p_cols, axis=1))
            sink_terms.append(jnp.where(lo, sink_cols[0], sink_cols[1]))
        return jnp.concatenate(p_rows, axis=0), sink_terms

    def values(a, g, p, sink_terms):
        rows = slice(a * w, (a + 1) * w)
        band = slice(a * w, (a + 2) * w)
        v2 = jnp.concatenate([jnp.concatenate([v_placed[g][0][band], ones_lo], axis=1),
                              jnp.concatenate([v_placed[g][1][band], ones_hi], axis=1)], axis=0)
        o2 = jnp.dot(p, v2, preferred_element_type=F32)
        for rb in range(2):
            sub = slice(rb * w, (rb + 1) * w)
            cols = slice((2 * g + rb) * LANES, (2 * g + rb + 1) * LANES)
            o_ref[0, rows, cols] = o2[sub, :LANES] / (o2[sub, LANES:] + sink_terms[rb])

    items = [(a, g) for a in range(SWA_TILE // w) for g in range(SWA_KV_HEADS)]
    scored, normed = {}, {}
    for step in range(len(items) + 2 * SWA_LAG):
        if step < len(items):
            scored[step] = scores(*items[step])
        mid, last = step - SWA_LAG, step - 2 * SWA_LAG
        if 0 <= mid < len(items):
            normed[mid] = softmax_numerators(items[mid][1], scored.pop(mid))
        if 0 <= last < len(items):
            values(*items[last], *normed.pop(last))

    for src_ref, dst_ref in zip(rest[:n_cast], rest[n_cast + 1:]):
        dst_ref[...] = src_ref[...].astype(BF16)


def _swa_attention(proj, sinks, weights):
    B, S, _ = proj.shape
    t = SWA_TILE
    w = WINDOW
    n_steps = B * (S // t)
    q_blk = (3 * SB_HEADS * HEAD_DIM) // (SWA_HEADS * HEAD_DIM)
    k_blk = (3 * SB_HEADS + SWA_HEADS) * HEAD_DIM // LANES
    v_blk = k_blk + 1
    prev = lambda n: jnp.maximum(n * (t // w) - 1, 0)
    bias = _swa_bias()

    def row_slice(weight):
        rows = weight.shape[0] // n_steps
        assert rows * n_steps == weight.shape[0] and rows % BF16_SUBLANES == 0
        return pl.BlockSpec((rows, weight.shape[1]), lambda b, n: (b * (S // t) + n, 0))

    cast_specs = [row_slice(weight) for weight in weights]
    outs = pl.pallas_call(
        _swa_kernel,
        grid=(B, S // t),
        in_specs=[
            pl.BlockSpec(memory_space=pltpu.SMEM),
            pl.BlockSpec(bias.shape, lambda b, n: (0, 0, 0, 0)),
            pl.BlockSpec((1, t, SWA_HEADS * HEAD_DIM), lambda b, n: (b, n, q_blk)),
            pl.BlockSpec((1, w, LANES), lambda b, n: (b, prev(n), k_blk)),
            pl.BlockSpec((1, t, LANES), lambda b, n: (b, n, k_blk)),
            pl.BlockSpec((1, w, LANES), lambda b, n: (b, prev(n), v_blk)),
            pl.BlockSpec((1, t, LANES), lambda b, n: (b, n, v_blk)),
        ] + cast_specs,
        out_specs=[pl.BlockSpec((1, t, SWA_HEADS * HEAD_DIM), lambda b, n: (b, n, 0))] + cast_specs,
        out_shape=[jax.ShapeDtypeStruct((B, S, SWA_HEADS * HEAD_DIM), F32)]
        + [jax.ShapeDtypeStruct(weight.shape, BF16) for weight in weights],
        compiler_params=pltpu.CompilerParams(
            dimension_semantics=("parallel", "arbitrary"), vmem_limit_bytes=VMEM_LIMIT),
        name="swa_attn",
    )(sinks, bias, proj, proj, proj, proj, proj, *weights)
    return outs[0], outs[1:]


def _ff_chunks(d_ff):
    n_tiles = d_ff // MXU_TILE
    per = FF_CHUNK // MXU_TILE
    sizes = [per] * (n_tiles // per) + ([n_tiles % per] if n_tiles % per else [])
    return tuple(s * MXU_TILE for s in sizes)


def _post_kernel(ysb_ref, ysw_ref, x_ref, mod_ref, wo_ref, gsb_ref, gsw_ref, l1g_ref, l1b_ref,
                 wgu_ref, wd_ref, l2g_ref, l2b_ref, o_ref, *, alpha, chunks):
    sb_w = ysb_ref.shape[-1]
    d_ff = wd_ref.shape[0]
    d = x_ref.shape[-1]
    batch_row = pl.ds(pl.program_id(0), 1)
    mod = lambda k: mod_ref[batch_row, k * d:(k + 1) * d]

    def attention_tail(rows):
        n_sb = _rms_norm(ysb_ref[0, rows, :], gsb_ref[...]).astype(BF16)
        n_sw = _rms_norm(ysw_ref[0, rows, :], gsw_ref[...]).astype(BF16)
        attn = (jnp.dot(n_sb, wo_ref[:sb_w, :], preferred_element_type=F32)
                + jnp.dot(n_sw, wo_ref[sb_w:, :], preferred_element_type=F32))
        return _layer_norm(alpha * x_ref[0, rows, :] + (1.0 + mod(2)) * attn,
                           l1g_ref[...], l1b_ref[...])

    def swiglu(x1):
        h = (x1 * (1.0 + mod(4)) + mod(3)).astype(BF16)
        ffn = None
        start = 0
        for width in chunks:
            gate = jnp.dot(h, wgu_ref[:, start:start + width], preferred_element_type=F32)
            up = jnp.dot(h, wgu_ref[:, d_ff + start:d_ff + start + width],
                         preferred_element_type=F32)
            a = (gate * jax.nn.sigmoid(gate) * up).astype(BF16)
            part = jnp.dot(a, wd_ref[start:start + width, :], preferred_element_type=F32)
            ffn = part if ffn is None else ffn + part
            start += width
        return ffn

    n_rows = x_ref.shape[1]
    groups = [slice(r, r + n_rows // POST_SPLIT) for r in range(0, n_rows, n_rows // POST_SPLIT)]
    x1s = [attention_tail(rows) for rows in groups]
    ffns = [swiglu(x1) for x1 in x1s]
    for rows, x1, ffn in zip(groups, x1s, ffns):
        o_ref[0, rows, :] = _layer_norm(alpha * x1 + (1.0 + mod(5)) * ffn,
                                        l2g_ref[...], l2b_ref[...])


def _post_attention(y_sb, y_sw, x, mod, w_out, gn_sb, gn_swa, ln1_g, ln1_b, w_gu, w_down,
                    ln2_g, ln2_b, alpha):
    B, S, D = x.shape
    W = y_sb.shape[-1]
    d_ff = w_down.shape[0]
    row = lambda b, i: (b, i, 0)
    const2 = lambda b, i: (0, 0)
    resident = lambda a: pl.BlockSpec(a.shape, const2, pipeline_mode=pl.Buffered(1))
    return pl.pallas_call(
        functools.partial(_post_kernel, alpha=alpha, chunks=_ff_chunks(d_ff)),
        grid=(B, S // POST_ROW_TILE),
        in_specs=[
            pl.BlockSpec((1, POST_ROW_TILE, W), row),
            pl.BlockSpec((1, POST_ROW_TILE, W), row),
            pl.BlockSpec((1, POST_ROW_TILE, D), row),
            resident(mod),
            resident(w_out), resident(gn_sb), resident(gn_swa), resident(ln1_g), resident(ln1_b),
            resident(w_gu), resident(w_down), resident(ln2_g), resident(ln2_b),
        ],
        out_specs=pl.BlockSpec((1, POST_ROW_TILE, D), row),
        out_shape=jax.ShapeDtypeStruct((B, S, D), F32),
        compiler_params=pltpu.CompilerParams(
            dimension_semantics=("parallel", "parallel"), vmem_limit_bytes=VMEM_LIMIT),
        name="post_attn",
    )(y_sb, y_sw, x, mod, w_out, gn_sb, gn_swa, ln1_g, ln1_b, w_gu, w_down, ln2_g, ln2_b)


def kernel(x, c, w_ada, b_ada, w_in, b_in, sinks, gn_sb, gn_swa, w_out, ln1_g, ln1_b,
           w_gu, w_down, ln2_g, ln2_b):
    depth = w_ada.shape[0]
    B, S, D = x.shape
    alpha = (2.0 * depth) ** 0.25
    sb_w = SB_HEADS * HEAD_DIM
    swa_qw = SWA_HEADS * HEAD_DIM
    d_in = w_in.shape[-1]
    cols = np.arange(d_in)
    is_q = (cols < sb_w) | ((cols >= 3 * sb_w) & (cols < 3 * sb_w + swa_qw))
    col_scale = jnp.asarray(np.where(is_q, LOG2_E / math.sqrt(HEAD_DIM), 1.0)[None, :], dtype=F32)
    tri = jnp.asarray(np.arange(SB_TILE)[:, None] >= np.arange(SB_TILE)[None, :], dtype=BF16)
    row2 = lambda a: a.reshape(1, -1)

    for l in range(depth):
        mod = _mod(c, w_ada[l], row2(b_ada[l]))
        proj = _in_proj(x, mod, w_in[l], row2(b_in[l]), col_scale)
        y_sb = _sb_attention(proj, tri)
        y_sw, (w_o, w_g, w_d) = _swa_attention(proj, sinks[l], [w_out[l], w_gu[l], w_down[l]])
        x = _post_attention(y_sb, y_sw, x, mod, w_o, row2(gn_sb[l]), row2(gn_swa[l]),
                            row2(ln1_g[l]), row2(ln1_b[l]), w_g, w_d, row2(ln2_g[l]),
                            row2(ln2_b[l]), alpha)
    return x
```

```python
import functools
import math

import jax
import jax.numpy as jnp
import numpy as np
from jax import lax
from jax.experimental import pallas as pl
from jax.experimental.pallas import tpu as pltpu

F32 = jnp.float32
BF16 = jnp.bfloat16

HEAD_DIM = 64
SB_HEADS = 8
SWA_HEADS = 8
SWA_KV_HEADS = 2
SWA_GROUP = SWA_HEADS // SWA_KV_HEADS
WINDOW = 128
LN_EPS = 1e-5
RMS_EPS = 1e-6
MASK_VALUE = -1e30
EXP2_UNDERFLOW = -150.0
LOG2_E = 1.4426950408889634
LANES = 128
BF16_SUBLANES = 16

SB_TILE = 256
SB_UNROLL = 4
SWA_TILE = 1024
SWA_LAG = 1
ROW_TILE = 1024
POST_ROW_TILE = 1024
POST_SPLIT = 4
MXU_TILE = 256
FF_CHUNK = 768
MOD_COLS = 1536
VMEM_LIMIT = 60 * 1024 * 1024


def _layer_norm(r, g, b):
    mu = jnp.mean(r, axis=-1, keepdims=True)
    d = r - mu
    var = jnp.mean(d * d, axis=-1, keepdims=True)
    return d * lax.rsqrt(var + LN_EPS) * g + b


def _rms_norm(y, g):
    ms = jnp.mean(y * y, axis=-1, keepdims=True)
    return y * lax.rsqrt(ms + RMS_EPS) * g


def _mod_kernel(ct_ref, wa_ref, wb_ref, b_ref, o_ref):
    ct = ct_ref[...]
    s = ct * jax.nn.sigmoid(ct)
    half = wa_ref.shape[0]
    for b in range(ct.shape[1]):
        col = s[:, b:b + 1]
        o_ref[b:b + 1, :] = (jnp.sum(wa_ref[...] * col[:half], axis=0, keepdims=True)
                             + jnp.sum(wb_ref[...] * col[half:], axis=0, keepdims=True)
                             + b_ref[...])


def _mod(c, w_ada, b_ada):
    B, D = c.shape
    N = w_ada.shape[1]
    return pl.pallas_call(
        _mod_kernel,
        grid=(N // MOD_COLS,),
        in_specs=[
            pl.BlockSpec((D, B), lambda n: (0, 0)),
            pl.BlockSpec((D // 2, MOD_COLS), lambda n: (0, n)),
            pl.BlockSpec((D // 2, MOD_COLS), lambda n: (1, n)),
            pl.BlockSpec((1, MOD_COLS), lambda n: (0, n)),
        ],
        out_specs=pl.BlockSpec((B, MOD_COLS), lambda n: (0, n)),
        out_shape=jax.ShapeDtypeStruct((B, N), F32),
        compiler_params=pltpu.CompilerParams(
            dimension_semantics=("arbitrary",), vmem_limit_bytes=VMEM_LIMIT),
        name="mod",
    )(c.T, w_ada, w_ada, b_ada)


def _inproj_kernel(x_ref, mod_ref, w_ref, b_ref, cs_ref, o_ref, wb_ref, *, col_chunk):
    @pl.when(jnp.logical_and(pl.program_id(0) == 0, pl.program_id(1) == 0))
    def _():
        wb_ref[...] = w_ref[...].astype(BF16)

    d = x_ref.shape[-1]
    batch_row = pl.ds(pl.program_id(0), 1)
    sh = mod_ref[batch_row, 0:d]
    sc = mod_ref[batch_row, d:2 * d]
    h = (x_ref[0] * (1.0 + sc) + sh).astype(BF16)
    n_cols = w_ref.shape[1]
    for n in range(n_cols // col_chunk):
        sl = slice(n * col_chunk, (n + 1) * col_chunk)
        p = jnp.dot(h, wb_ref[:, sl], preferred_element_type=F32)
        o_ref[0, :, sl] = ((p + b_ref[:, sl]) * cs_ref[:, sl]).astype(BF16)


def _in_proj(x, mod, w_in, b_in, col_scale):
    B, S, D = x.shape
    N = w_in.shape[1]
    return pl.pallas_call(
        functools.partial(_inproj_kernel, col_chunk=768),
        grid=(B, S // ROW_TILE),
        in_specs=[
            pl.BlockSpec((1, ROW_TILE, D), lambda b, i: (b, i, 0)),
            pl.BlockSpec(mod.shape, lambda b, i: (0, 0)),
            pl.BlockSpec((D, N), lambda b, i: (0, 0), pipeline_mode=pl.Buffered(1)),
            pl.BlockSpec((1, N), lambda b, i: (0, 0)),
            pl.BlockSpec((1, N), lambda b, i: (0, 0)),
        ],
        out_specs=pl.BlockSpec((1, ROW_TILE, N), lambda b, i: (b, i, 0)),
        out_shape=jax.ShapeDtypeStruct((B, S, N), BF16),
        scratch_shapes=[pltpu.VMEM((D, N), BF16)],
        compiler_params=pltpu.CompilerParams(
            dimension_semantics=("arbitrary", "arbitrary"), vmem_limit_bytes=VMEM_LIMIT),
        name="in_proj",
    )(x, mod, w_in, b_in, col_scale)


def _sb_kernel(q_ref, k_ref, v_ref, u_ref, o_ref):
    t = SB_TILE
    n_tiles = q_ref.shape[1] // t
    u = u_ref[...]
    first_head = lambda rows: lax.broadcasted_iota(jnp.int32, (rows, LANES), 1) < HEAD_DIM
    row = lax.broadcasted_iota(jnp.int32, (2 * t, t), 0)
    col = lax.broadcasted_iota(jnp.int32, (2 * t, t), 1)
    before = col < (row & (t - 1))
    nt_dims = (((1,), (1,)), ((), ()))

    def log_one_minus_sigmoid(z):
        neg_part = jnp.minimum(z, 0.0)
        neg_relu = neg_part - z
        return neg_relu - jnp.log2(1.0 + jnp.exp2(neg_part + neg_relu))

    def suffix(lr):
        return jnp.dot(lr.astype(BF16), u, preferred_element_type=F32)

    def row_total(lr):
        return jnp.sum(lr, axis=-1, keepdims=True)

    def weighted_values(w, vals):
        w = w.astype(BF16)
        zero = jnp.zeros_like(vals)
        in_first = first_head(vals.shape[0])
        w2 = jnp.concatenate([w[:t], w[t:]], axis=1)
        v2 = jnp.concatenate([jnp.where(in_first, vals, zero), jnp.where(in_first, zero, vals)],
                             axis=0)
        return jnp.dot(w2, v2, preferred_element_type=F32)

    def score_diagonal_and_previous(i):
        here = pl.multiple_of(i * t, t)
        prev = pl.multiple_of(jnp.maximum(i - 1, 0) * t, t)
        q = q_ref[0, pl.ds(here, t), :]
        zero = jnp.zeros_like(q)
        in_first = first_head(t)
        qs = jnp.concatenate([jnp.where(in_first, q, zero), jnp.where(in_first, zero, q)], axis=0)

        keys = jnp.concatenate([k_ref[0, pl.ds(prev, t), :], k_ref[0, pl.ds(here, t), :]], axis=0)
        vals = jnp.concatenate([v_ref[0, pl.ds(prev, t), :], v_ref[0, pl.ds(here, t), :]], axis=0)
        z = lax.dot_general(qs, keys, nt_dims, preferred_element_type=F32)
        return qs, vals, z, log_one_minus_sigmoid(z)

    def weigh_diagonal_and_previous(i, qs, vals, z, lr):
        lr_p, lr_d = lr[:, :t], jnp.where(before, lr[:, t:], 0.0)
        run_p = row_total(lr_d) + jnp.where(i == 0, MASK_VALUE, 0.0)
        w_d = jnp.where(before, jnp.exp2(z[:, t:] + suffix(lr_d)), 0.0)
        w_p = jnp.exp2(z[:, :t] + suffix(lr_p) + run_p)
        acc = weighted_values(jnp.concatenate([w_p, w_d], axis=1), vals)
        run = run_p + row_total(lr_p)
        return qs, run, acc, jnp.max(run) >= EXP2_UNDERFLOW

    def earlier_tiles(i, qs, run, acc, live):
        def cond(c):
            j, live, _, _ = c
            return jnp.logical_and(j >= 0, live)

        def body(c):
            j, _, run, acc = c
            start = pl.multiple_of(j * t, t)
            z = lax.dot_general(qs, k_ref[0, pl.ds(start, t), :], nt_dims,
                                preferred_element_type=F32)
            lr = log_one_minus_sigmoid(z)
            w = jnp.exp2(z + suffix(lr) + run)
            acc = acc + weighted_values(w, v_ref[0, pl.ds(start, t), :])
            run = run + row_total(lr)
            return j - 1, jnp.max(run) >= EXP2_UNDERFLOW, run, acc

        _, _, _, acc = lax.while_loop(cond, body, (i - 2, live, run, acc))
        o_ref[0, pl.ds(pl.multiple_of(i * t, t), t), :] = acc

    def q_tiles(g, carry):
        tiles = [g * SB_UNROLL + s for s in range(SB_UNROLL)]
        heads, scored = [], score_diagonal_and_previous(tiles[0])
        for s, i in enumerate(tiles):
            ahead = score_diagonal_and_previous(tiles[s + 1]) if s + 1 < len(tiles) else None
            heads.append(weigh_diagonal_and_previous(i, *scored))
            scored = ahead
        for i, head in zip(tiles, heads):
            earlier_tiles(i, *head)
        return carry

    lax.fori_loop(0, n_tiles // SB_UNROLL, q_tiles, 0)


def _sb_attention(proj, u):
    B, S, _ = proj.shape
    n_pairs = SB_HEADS * HEAD_DIM // LANES
    seq = lambda blk: pl.BlockSpec((1, S, LANES), lambda b, p: (b, 0, blk * n_pairs + p))
    return pl.pallas_call(
        _sb_kernel,
        grid=(B, n_pairs),
        in_specs=[seq(0), seq(1), seq(2), pl.BlockSpec(u.shape, lambda b, p: (0, 0))],
        out_specs=pl.BlockSpec((1, S, LANES), lambda b, p: (b, 0, p)),
        out_shape=jax.ShapeDtypeStruct((B, S, SB_HEADS * HEAD_DIM), F32),
        compiler_params=pltpu.CompilerParams(
            dimension_semantics=("parallel", "parallel"), vmem_limit_bytes=VMEM_LIMIT),
        name="sb_attn",
    )(proj, proj, proj, u)


def _swa_bias():
    w = WINDOW
    r = np.arange(2 * w)[:, None]
    c = np.arange(4 * w)[None, :]
    dist = (r % w) + w - (c % (2 * w))
    in_band = (dist >= 0) & (dist < w)
    groups = []
    for g in range(SWA_KV_HEADS):
        head = SWA_GROUP * g + 2 * (r // w) + (c // (2 * w))
        slope = np.exp2(-8.0 * (head + 1) / SWA_HEADS)
        groups.append(np.where(in_band, -slope * dist * LOG2_E, MASK_VALUE))
    rest = np.stack(groups)
    first = np.where((c % (2 * w)) >= w, rest, MASK_VALUE)
    return jnp.asarray(np.stack([first, rest]), dtype=F32)


def _swa_kernel(sink_ref, bias_ref, q_ref, kp_ref, kc_ref, vp_ref, vc_ref, *rest):
    w = WINDOW
    n_cast = (len(rest) - 1) // 2
    o_ref = rest[n_cast]
    n = pl.program_id(1)
    lane = lax.broadcasted_iota(jnp.int32, (w, LANES), 1)
    lo = lane < HEAD_DIM
    nt_dims = (((1,), (1,)), ((), ()))

    def placed(prev_ref, cur_ref):
        band = jnp.concatenate([prev_ref[0], cur_ref[0]], axis=0).astype(F32)
        swapped = pltpu.roll(band, HEAD_DIM, axis=1)
        in_lo = lax.broadcasted_iota(jnp.int32, band.shape, 1) < HEAD_DIM
        return [
            (jnp.where(in_lo, band, 0.0).astype(BF16), jnp.where(in_lo, 0.0, swapped).astype(BF16)),
            (jnp.where(in_lo, swapped, 0.0).astype(BF16), jnp.where(in_lo, 0.0, band).astype(BF16)),
        ]

    k_placed = placed(kp_ref, kc_ref)
    v_placed = placed(vp_ref, vc_ref)
    band_lo = lax.broadcasted_iota(jnp.int32, (2 * w, LANES), 1) < HEAD_DIM
    ones_lo = jnp.where(band_lo, 1.0, 0.0).astype(BF16)
    ones_hi = jnp.where(band_lo, 0.0, 1.0).astype(BF16)
    variant = jnp.minimum(n, 1)

    def scores(a, g):
        rows = slice(a * w, (a + 1) * w)
        band = slice(a * w, (a + 2) * w)
        cols = [slice((2 * g + rb) * LANES, (2 * g + rb + 1) * LANES) for rb in range(2)]
        q2 = jnp.concatenate([q_ref[0, rows, cols[0]], q_ref[0, rows, cols[1]]], axis=0)
        k2 = jnp.concatenate([k_placed[g][0][band], k_placed[g][1][band]], axis=0)
        bias = bias_ref[variant, g] if a == 0 else bias_ref[1, g]
        return lax.dot_general(q2, k2, nt_dims, preferred_element_type=F32) + bias

    def softmax_numerators(g, s):
        p_rows, sink_terms = [], []
        for rb in range(2):
            p_cols, sink_cols = [], []
            for half in range(2):
                sink = sink_ref[SWA_GROUP * g + 2 * rb + half] * LOG2_E
                sq = s[rb * w:(rb + 1) * w, half * 2 * w:(half + 1) * 2 * w]
                m = jnp.maximum(jnp.max(sq, axis=-1, keepdims=True), sink)
                p_cols.append(jnp.exp2(sq - m).astype(BF16))
                sink_cols.append(jnp.exp2(sink - m))
            p_rows.append(jnp.concatenate(p_cols, axis=1))
            sink_terms.append(jnp.where(lo, sink_cols[0], sink_cols[1]))
        return jnp.concatenate(p_rows, axis=0), sink_terms

    def values(a, g, p, sink_terms):
        rows = slice(a * w, (a + 1) * w)
        band = slice(a * w, (a + 2) * w)
        v2 = jnp.concatenate([jnp.concatenate([v_placed[g][0][band], ones_lo], axis=1),
                              jnp.concatenate([v_placed[g][1][band], ones_hi], axis=1)], axis=0)
        o2 = jnp.dot(p, v2, preferred_element_type=F32)
        for rb in range(2):
            sub = slice(rb * w, (rb + 1) * w)
            cols = slice((2 * g + rb) * LANES, (2 * g + rb + 1) * LANES)
            o_ref[0, rows, cols] = o2[sub, :LANES] / (o2[sub, LANES:] + sink_terms[rb])

    items = [(a, g) for a in range(SWA_TILE // w) for g in range(SWA_KV_HEADS)]
    scored, normed = {}, {}
    for step in range(len(items) + 2 * SWA_LAG):
        if step < len(items):
            scored[step] = scores(*items[step])
        mid, last = step - SWA_LAG, step - 2 * SWA_LAG
        if 0 <= mid < len(items):
            normed[mid] = softmax_numerators(items[mid][1], scored.pop(mid))
        if 0 <= last < len(items):
            values(*items[last], *normed.pop(last))

    for src_ref, dst_ref in zip(rest[:n_cast], rest[n_cast + 1:]):
        dst_ref[...] = src_ref[...].astype(BF16)


def _swa_attention(proj, sinks, weights):
    B, S, _ = proj.shape
    t = SWA_TILE
    w = WINDOW
    n_steps = B * (S // t)
    q_blk = (3 * SB_HEADS * HEAD_DIM) // (SWA_HEADS * HEAD_DIM)
    k_blk = (3 * SB_HEADS + SWA_HEADS) * HEAD_DIM // LANES
    v_blk = k_blk + 1
    prev = lambda n: jnp.maximum(n * (t // w) - 1, 0)
    bias = _swa_bias()

    def row_slice(weight):
        rows = weight.shape[0] // n_steps
        assert rows * n_steps == weight.shape[0] and rows % BF16_SUBLANES == 0
        return pl.BlockSpec((rows, weight.shape[1]), lambda b, n: (b * (S // t) + n, 0))

    cast_specs = [row_slice(weight) for weight in weights]
    outs = pl.pallas_call(
        _swa_kernel,
        grid=(B, S // t),
        in_specs=[
            pl.BlockSpec(memory_space=pltpu.SMEM),
            pl.BlockSpec(bias.shape, lambda b, n: (0, 0, 0, 0)),
            pl.BlockSpec((1, t, SWA_HEADS * HEAD_DIM), lambda b, n: (b, n, q_blk)),
            pl.BlockSpec((1, w, LANES), lambda b, n: (b, prev(n), k_blk)),
            pl.BlockSpec((1, t, LANES), lambda b, n: (b, n, k_blk)),
            pl.BlockSpec((1, w, LANES), lambda b, n: (b, prev(n), v_blk)),
            pl.BlockSpec((1, t, LANES), lambda b, n: (b, n, v_blk)),
        ] + cast_specs,
        out_specs=[pl.BlockSpec((1, t, SWA_HEADS * HEAD_DIM), lambda b, n: (b, n, 0))] + cast_specs,
        out_shape=[jax.ShapeDtypeStruct((B, S, SWA_HEADS * HEAD_DIM), F32)]
        + [jax.ShapeDtypeStruct(weight.shape, BF16) for weight in weights],
        compiler_params=pltpu.CompilerParams(
            dimension_semantics=("parallel", "arbitrary"), vmem_limit_bytes=VMEM_LIMIT),
        name="swa_attn",
    )(sinks, bias, proj, proj, proj, proj, proj, *weights)
    return outs[0], outs[1:]


def _ff_chunks(d_ff):
    n_tiles = d_ff // MXU_TILE
    per = FF_CHUNK // MXU_TILE
    sizes = [per] * (n_tiles // per) + ([n_tiles % per] if n_tiles % per else [])
    return tuple(s * MXU_TILE for s in sizes)


def _post_kernel(ysb_ref, ysw_ref, x_ref, mod_ref, wo_ref, gsb_ref, gsw_ref, l1g_ref, l1b_ref,
                 wgu_ref, wd_ref, l2g_ref, l2b_ref, o_ref, *, alpha, chunks):
    sb_w = ysb_ref.shape[-1]
    d_ff = wd_ref.shape[0]
    d = x_ref.shape[-1]
    batch_row = pl.ds(pl.program_id(0), 1)
    mod = lambda k: mod_ref[batch_row, k * d:(k + 1) * d]

    def attention_tail(rows):
        n_sb = _rms_norm(ysb_ref[0, rows, :], gsb_ref[...]).astype(BF16)
        n_sw = _rms_norm(ysw_ref[0, rows, :], gsw_ref[...]).astype(BF16)
        attn = (jnp.dot(n_sb, wo_ref[:sb_w, :], preferred_element_type=F32)
                + jnp.dot(n_sw, wo_ref[sb_w:, :], preferred_element_type=F32))
        return _layer_norm(alpha * x_ref[0, rows, :] + (1.0 + mod(2)) * attn,
                           l1g_ref[...], l1b_ref[...])

    def swiglu(x1):
        h = (x1 * (1.0 + mod(4)) + mod(3)).astype(BF16)
        ffn = None
        start = 0
        for width in chunks:
            gate = jnp.dot(h, wgu_ref[:, start:start + width], preferred_element_type=F32)
            up = jnp.dot(h, wgu_ref[:, d_ff + start:d_ff + start + width],
                         preferred_element_type=F32)
            a = (gate * jax.nn.sigmoid(gate) * up).astype(BF16)
            part = jnp.dot(a, wd_ref[start:start + width, :], preferred_element_type=F32)
            ffn = part if ffn is None else ffn + part
            start += width
        return ffn

    n_rows = x_ref.shape[1]
    groups = [slice(r, r + n_rows // POST_SPLIT) for r in range(0, n_rows, n_rows // POST_SPLIT)]
    x1s = [attention_tail(rows) for rows in groups]
    ffns = [swiglu(x1) for x1 in x1s]
    for rows, x1, ffn in zip(groups, x1s, ffns):
        o_ref[0, rows, :] = _layer_norm(alpha * x1 + (1.0 + mod(5)) * ffn,
                                        l2g_ref[...], l2b_ref[...])


def _post_attention(y_sb, y_sw, x, mod, w_out, gn_sb, gn_swa, ln1_g, ln1_b, w_gu, w_down,
                    ln2_g, ln2_b, alpha):
    B, S, D = x.shape
    W = y_sb.shape[-1]
    d_ff = w_down.shape[0]
    row = lambda b, i: (b, i, 0)
    const2 = lambda b, i: (0, 0)
    resident = lambda a: pl.BlockSpec(a.shape, const2, pipeline_mode=pl.Buffered(1))
    return pl.pallas_call(
        functools.partial(_post_kernel, alpha=alpha, chunks=_ff_chunks(d_ff)),
        grid=(B, S // POST_ROW_TILE),
        in_specs=[
            pl.BlockSpec((1, POST_ROW_TILE, W), row),
            pl.BlockSpec((1, POST_ROW_TILE, W), row),
            pl.BlockSpec((1, POST_ROW_TILE, D), row),
            resident(mod),
            resident(w_out), resident(gn_sb), resident(gn_swa), resident(ln1_g), resident(ln1_b),
            resident(w_gu), resident(w_down), resident(ln2_g), resident(ln2_b),
        ],
        out_specs=pl.BlockSpec((1, POST_ROW_TILE, D), row),
        out_shape=jax.ShapeDtypeStruct((B, S, D), F32),
        compiler_params=pltpu.CompilerParams(
            dimension_semantics=("parallel", "parallel"), vmem_limit_bytes=VMEM_LIMIT),
        name="post_attn",
    )(y_sb, y_sw, x, mod, w_out, gn_sb, gn_swa, ln1_g, ln1_b, w_gu, w_down, ln2_g, ln2_b)


def kernel(x, c, w_ada, b_ada, w_in, b_in, sinks, gn_sb, gn_swa, w_out, ln1_g, ln1_b,
           w_gu, w_down, ln2_g, ln2_b):
    depth = w_ada.shape[0]
    B, S, D = x.shape
    alpha = (2.0 * depth) ** 0.25
    sb_w = SB_HEADS * HEAD_DIM
    swa_qw = SWA_HEADS * HEAD_DIM
    d_in = w_in.shape[-1]
    cols = np.arange(d_in)
    is_q = (cols < sb_w) | ((cols >= 3 * sb_w) & (cols < 3 * sb_w + swa_qw))
    col_scale = jnp.asarray(np.where(is_q, LOG2_E / math.sqrt(HEAD_DIM), 1.0)[None, :], dtype=F32)
    tri = jnp.asarray(np.arange(SB_TILE)[:, None] >= np.arange(SB_TILE)[None, :], dtype=BF16)
    row2 = lambda a: a.reshape(1, -1)

    for l in range(depth):
        mod = _mod(c, w_ada[l], row2(b_ada[l]))
        proj = _in_proj(x, mod, w_in[l], row2(b_in[l]), col_scale)
        y_sb = _sb_attention(proj, tri)
        y_sw, (w_o, w_g, w_d) = _swa_attention(proj, sinks[l], [w_out[l], w_gu[l], w_down[l]])
        x = _post_attention(y_sb, y_sw, x, mod, w_o, row2(gn_sb[l]), row2(gn_swa[l]),
                            row2(ln1_g[l]), row2(ln1_b[l]), w_g, w_d, row2(ln2_g[l]),
                            row2(ln2_b[l]), alpha)
    return x
```

```python
import functools
import math

import jax
import jax.numpy as jnp
import numpy as np
from jax import lax
from jax.experimental import pallas as pl
from jax.experimental.pallas import tpu as pltpu

F32 = jnp.float32
BF16 = jnp.bfloat16

HEAD_DIM = 64
SB_HEADS = 8
SWA_HEADS = 8
SWA_KV_HEADS = 2
SWA_GROUP = SWA_HEADS // SWA_KV_HEADS
WINDOW = 128
LN_EPS = 1e-5
RMS_EPS = 1e-6
MASK_VALUE = -1e30
EXP2_UNDERFLOW = -150.0
LOG2_E = 1.4426950408889634
LANES = 128
BF16_SUBLANES = 16

SB_TILE = 256
SB_UNROLL = 4
SWA_TILE = 1024
SWA_LAG = 1
ROW_TILE = 1024
POST_ROW_TILE = 1024
POST_SPLIT = 4
MXU_TILE = 256
FF_CHUNK = 1536
MOD_COLS = 1536
VMEM_LIMIT = 60 * 1024 * 1024


def _layer_norm(r, g, b):
    mu = jnp.mean(r, axis=-1, keepdims=True)
    d = r - mu
    var = jnp.mean(d * d, axis=-1, keepdims=True)
    return d * lax.rsqrt(var + LN_EPS) * g + b


def _rms_norm(y, g):
    ms = jnp.mean(y * y, axis=-1, keepdims=True)
    return y * lax.rsqrt(ms + RMS_EPS) * g


def _mod_kernel(ct_ref, wa_ref, wb_ref, b_ref, o_ref):
    ct = ct_ref[...]
    s = ct * jax.nn.sigmoid(ct)
    half = wa_ref.shape[0]
    for b in range(ct.shape[1]):
        col = s[:, b:b + 1]
        o_ref[b:b + 1, :] = (jnp.sum(wa_ref[...] * col[:half], axis=0, keepdims=True)
                             + jnp.sum(wb_ref[...] * col[half:], axis=0, keepdims=True)
                             + b_ref[...])


def _mod(c, w_ada, b_ada):
    B, D = c.shape
    N = w_ada.shape[1]
    return pl.pallas_call(
        _mod_kernel,
        grid=(N // MOD_COLS,),
        in_specs=[
            pl.BlockSpec((D, B), lambda n: (0, 0)),
            pl.BlockSpec((D // 2, MOD_COLS), lambda n: (0, n)),
            pl.BlockSpec((D // 2, MOD_COLS), lambda n: (1, n)),
            pl.BlockSpec((1, MOD_COLS), lambda n: (0, n)),
        ],
        out_specs=pl.BlockSpec((B, MOD_COLS), lambda n: (0, n)),
        out_shape=jax.ShapeDtypeStruct((B, N), F32),
        compiler_params=pltpu.CompilerParams(
            dimension_semantics=("arbitrary",), vmem_limit_bytes=VMEM_LIMIT),
        name="mod",
    )(c.T, w_ada, w_ada, b_ada)


def _inproj_kernel(x_ref, mod_ref, w_ref, b_ref, cs_ref, o_ref, wb_ref, *, col_chunk):
    @pl.when(jnp.logical_and(pl.program_id(0) == 0, pl.program_id(1) == 0))
    def _():
        wb_ref[...] = w_ref[...].astype(BF16)

    d = x_ref.shape[-1]
    batch_row = pl.ds(pl.program_id(0), 1)
    sh = mod_ref[batch_row, 0:d]
    sc = mod_ref[batch_row, d:2 * d]
    h = (x_ref[0] * (1.0 + sc) + sh).astype(BF16)
    n_cols = w_ref.shape[1]
    for n in range(n_cols // col_chunk):
        sl = slice(n * col_chunk, (n + 1) * col_chunk)
        p = jnp.dot(h, wb_ref[:, sl], preferred_element_type=F32)
        o_ref[0, :, sl] = ((p + b_ref[:, sl]) * cs_ref[:, sl]).astype(BF16)


def _in_proj(x, mod, w_in, b_in, col_scale):
    B, S, D = x.shape
    N = w_in.shape[1]
    return pl.pallas_call(
        functools.partial(_inproj_kernel, col_chunk=768),
        grid=(B, S // ROW_TILE),
        in_specs=[
            pl.BlockSpec((1, ROW_TILE, D), lambda b, i: (b, i, 0)),
            pl.BlockSpec(mod.shape, lambda b, i: (0, 0)),
            pl.BlockSpec((D, N), lambda b, i: (0, 0), pipeline_mode=pl.Buffered(1)),
            pl.BlockSpec((1, N), lambda b, i: (0, 0)),
            pl.BlockSpec((1, N), lambda b, i: (0, 0)),
        ],
        out_specs=pl.BlockSpec((1, ROW_TILE, N), lambda b, i: (b, i, 0)),
        out_shape=jax.ShapeDtypeStruct((B, S, N), BF16),
        scratch_shapes=[pltpu.VMEM((D, N), BF16)],
        compiler_params=pltpu.CompilerParams(
            dimension_semantics=("arbitrary", "arbitrary"), vmem_limit_bytes=VMEM_LIMIT),
        name="in_proj",
    )(x, mod, w_in, b_in, col_scale)


def _sb_kernel(q_ref, k_ref, v_ref, u_ref, o_ref):
    t = SB_TILE
    n_tiles = q_ref.shape[1] // t
    u = u_ref[...]
    first_head = lambda rows: lax.broadcasted_iota(jnp.int32, (rows, LANES), 1) < HEAD_DIM
    row = lax.broadcasted_iota(jnp.int32, (2 * t, t), 0)
    col = lax.broadcasted_iota(jnp.int32, (2 * t, t), 1)
    before = col < (row & (t - 1))
    nt_dims = (((1,), (1,)), ((), ()))

    def log_one_minus_sigmoid(z):
        neg_part = jnp.minimum(z, 0.0)
        neg_relu = neg_part - z
        return neg_relu - jnp.log2(1.0 + jnp.exp2(neg_part + neg_relu))

    def suffix(lr):
        return jnp.dot(lr.astype(BF16), u, preferred_element_type=F32)

    def row_total(lr):
        return jnp.sum(lr, axis=-1, keepdims=True)

    def weighted_values(w, vals):
        w = w.astype(BF16)
        zero = jnp.zeros_like(vals)
        in_first = first_head(vals.shape[0])
        w2 = jnp.concatenate([w[:t], w[t:]], axis=1)
        v2 = jnp.concatenate([jnp.where(in_first, vals, zero), jnp.where(in_first, zero, vals)],
                             axis=0)
        return jnp.dot(w2, v2, preferred_element_type=F32)

    def score_diagonal_and_previous(i):
        here = pl.multiple_of(i * t, t)
        prev = pl.multiple_of(jnp.maximum(i - 1, 0) * t, t)
        q = q_ref[0, pl.ds(here, t), :]
        zero = jnp.zeros_like(q)
        in_first = first_head(t)
        qs = jnp.concatenate([jnp.where(in_first, q, zero), jnp.where(in_first, zero, q)], axis=0)

        keys = jnp.concatenate([k_ref[0, pl.ds(prev, t), :], k_ref[0, pl.ds(here, t), :]], axis=0)
        vals = jnp.concatenate([v_ref[0, pl.ds(prev, t), :], v_ref[0, pl.ds(here, t), :]], axis=0)
        z = lax.dot_general(qs, keys, nt_dims, preferred_element_type=F32)
        return qs, vals, z, log_one_minus_sigmoid(z)

    def weigh_diagonal_and_previous(i, qs, vals, z, lr):
        lr_p, lr_d = lr[:, :t], jnp.where(before, lr[:, t:], 0.0)
        run_p = row_total(lr_d) + jnp.where(i == 0, MASK_VALUE, 0.0)
        w_d = jnp.where(before, jnp.exp2(z[:, t:] + suffix(lr_d)), 0.0)
        w_p = jnp.exp2(z[:, :t] + suffix(lr_p) + run_p)
        acc = weighted_values(jnp.concatenate([w_p, w_d], axis=1), vals)
        run = run_p + row_total(lr_p)
        return qs, run, acc, jnp.max(run) >= EXP2_UNDERFLOW

    def earlier_tiles(i, qs, run, acc, live):
        def cond(c):
            j, live, _, _ = c
            return jnp.logical_and(j >= 0, live)

        def body(c):
            j, _, run, acc = c
            start = pl.multiple_of(j * t, t)
            z = lax.dot_general(qs, k_ref[0, pl.ds(start, t), :], nt_dims,
                                preferred_element_type=F32)
            lr = log_one_minus_sigmoid(z)
            w = jnp.exp2(z + suffix(lr) + run)
            acc = acc + weighted_values(w, v_ref[0, pl.ds(start, t), :])
            run = run + row_total(lr)
            return j - 1, jnp.max(run) >= EXP2_UNDERFLOW, run, acc

        _, _, _, acc = lax.while_loop(cond, body, (i - 2, live, run, acc))
        o_ref[0, pl.ds(pl.multiple_of(i * t, t), t), :] = acc

    def q_tiles(g, carry):
        tiles = [g * SB_UNROLL + s for s in range(SB_UNROLL)]
        heads, scored = [], score_diagonal_and_previous(tiles[0])
        for s, i in enumerate(tiles):
            ahead = score_diagonal_and_previous(tiles[s + 1]) if s + 1 < len(tiles) else None
            heads.append(weigh_diagonal_and_previous(i, *scored))
            scored = ahead
        for i, head in zip(tiles, heads):
            earlier_tiles(i, *head)
        return carry

    lax.fori_loop(0, n_tiles // SB_UNROLL, q_tiles, 0)


def _sb_attention(proj, u):
    B, S, _ = proj.shape
    n_pairs = SB_HEADS * HEAD_DIM // LANES
    seq = lambda blk: pl.BlockSpec((1, S, LANES), lambda b, p: (b, 0, blk * n_pairs + p))
    return pl.pallas_call(
        _sb_kernel,
        grid=(B, n_pairs),
        in_specs=[seq(0), seq(1), seq(2), pl.BlockSpec(u.shape, lambda b, p: (0, 0))],
        out_specs=pl.BlockSpec((1, S, LANES), lambda b, p: (b, 0, p)),
        out_shape=jax.ShapeDtypeStruct((B, S, SB_HEADS * HEAD_DIM), F32),
        compiler_params=pltpu.CompilerParams(
            dimension_semantics=("parallel", "parallel"), vmem_limit_bytes=VMEM_LIMIT),
        name="sb_attn",
    )(proj, proj, proj, u)


def _swa_bias():
    w = WINDOW
    r = np.arange(2 * w)[:, None]
    c = np.arange(4 * w)[None, :]
    dist = (r % w) + w - (c % (2 * w))
    in_band = (dist >= 0) & (dist < w)
    groups = []
    for g in range(SWA_KV_HEADS):
        head = SWA_GROUP * g + 2 * (r // w) + (c // (2 * w))
        slope = np.exp2(-8.0 * (head + 1) / SWA_HEADS)
        groups.append(np.where(in_band, -slope * dist * LOG2_E, MASK_VALUE))
    rest = np.stack(groups)
    first = np.where((c % (2 * w)) >= w, rest, MASK_VALUE)
    return jnp.asarray(np.stack([first, rest]), dtype=F32)


def _swa_kernel(sink_ref, bias_ref, q_ref, kp_ref, kc_ref, vp_ref, vc_ref, *rest):
    w = WINDOW
    n_cast = (len(rest) - 1) // 2
    o_ref = rest[n_cast]
    n = pl.program_id(1)
    lane = lax.broadcasted_iota(jnp.int32, (w, LANES), 1)
    lo = lane < HEAD_DIM
    nt_dims = (((1,), (1,)), ((), ()))

    def placed(prev_ref, cur_ref):
        band = jnp.concatenate([prev_ref[0], cur_ref[0]], axis=0).astype(F32)
        swapped = pltpu.roll(band, HEAD_DIM, axis=1)
        in_lo = lax.broadcasted_iota(jnp.int32, band.shape, 1) < HEAD_DIM
        return [
            (jnp.where(in_lo, band, 0.0).astype(BF16), jnp.where(in_lo, 0.0, swapped).astype(BF16)),
            (jnp.where(in_lo, swapped, 0.0).astype(BF16), jnp.where(in_lo, 0.0, band).astype(BF16)),
        ]

    k_placed = placed(kp_ref, kc_ref)
    v_placed = placed(vp_ref, vc_ref)
    band_lo = lax.broadcasted_iota(jnp.int32, (2 * w, LANES), 1) < HEAD_DIM
    ones_lo = jnp.where(band_lo, 1.0, 0.0).astype(BF16)
    ones_hi = jnp.where(band_lo, 0.0, 1.0).astype(BF16)
    variant = jnp.minimum(n, 1)

    def scores(a, g):
        rows = slice(a * w, (a + 1) * w)
        band = slice(a * w, (a + 2) * w)
        cols = [slice((2 * g + rb) * LANES, (2 * g + rb + 1) * LANES) for rb in range(2)]
        q2 = jnp.concatenate([q_ref[0, rows, cols[0]], q_ref[0, rows, cols[1]]], axis=0)
        k2 = jnp.concatenate([k_placed[g][0][band], k_placed[g][1][band]], axis=0)
        bias = bias_ref[variant, g] if a == 0 else bias_ref[1, g]
        return lax.dot_general(q2, k2, nt_dims, preferred_element_type=F32) + bias

    def softmax_numerators(g, s):
        p_rows, sink_terms = [], []
        for rb in range(2):
            p_cols, sink_cols = [], []
            for half in range(2):
                sink = sink_ref[SWA_GROUP * g + 2 * rb + half] * LOG2_E
                sq = s[rb * w:(rb + 1) * w, half * 2 * w:(half + 1) * 2 * w]
                m = jnp.maximum(jnp.max(sq, axis=-1, keepdims=True), sink)
                p_cols.append(jnp.exp2(sq - m).astype(BF16))
                sink_cols.append(jnp.exp2(sink - m))
            p_rows.append(jnp.concatenate(p_cols, axis=1))
            sink_terms.append(jnp.where(lo, sink_cols[0], sink_cols[1]))
        return jnp.concatenate(p_rows, axis=0), sink_terms

    def values(a, g, p, sink_terms):
        rows = slice(a * w, (a + 1) * w)
        band = slice(a * w, (a + 2) * w)
        v2 = jnp.concatenate([jnp.concatenate([v_placed[g][0][band], ones_lo], axis=1),
                              jnp.concatenate([v_placed[g][1][band], ones_hi], axis=1)], axis=0)
        o2 = jnp.dot(p, v2, preferred_element_type=F32)
        for rb in range(2):
            sub = slice(rb * w, (rb + 1) * w)
            cols = slice((2 * g + rb) * LANES, (2 * g + rb + 1) * LANES)
            o_ref[0, rows, cols] = o2[sub, :LANES] / (o2[sub, LANES:] + sink_terms[rb])

    items = [(a, g) for a in range(SWA_TILE // w) for g in range(SWA_KV_HEADS)]
    scored, normed = {}, {}
    for step in range(len(items) + 2 * SWA_LAG):
        if step < len(items):
            scored[step] = scores(*items[step])
        mid, last = step - SWA_LAG, step - 2 * SWA_LAG
        if 0 <= mid < len(items):
            normed[mid] = softmax_numerators(items[mid][1], scored.pop(mid))
        if 0 <= last < len(items):
            values(*items[last], *normed.pop(last))

    for src_ref, dst_ref in zip(rest[:n_cast], rest[n_cast + 1:]):
        dst_ref[...] = src_ref[...].astype(BF16)


def _swa_attention(proj, sinks, weights):
    B, S, _ = proj.shape
    t = SWA_TILE
    w = WINDOW
    n_steps = B * (S // t)
    q_blk = (3 * SB_HEADS * HEAD_DIM) // (SWA_HEADS * HEAD_DIM)
    k_blk = (3 * SB_HEADS + SWA_HEADS) * HEAD_DIM // LANES
    v_blk = k_blk + 1
    prev = lambda n: jnp.maximum(n * (t // w) - 1, 0)
    bias = _swa_bias()

    def row_slice(weight):
        rows = weight.shape[0] // n_steps
        assert rows * n_steps == weight.shape[0] and rows % BF16_SUBLANES == 0
        return pl.BlockSpec((rows, weight.shape[1]), lambda b, n: (b * (S // t) + n, 0))

    cast_specs = [row_slice(weight) for weight in weights]
    outs = pl.pallas_call(
        _swa_kernel,
        grid=(B, S // t),
        in_specs=[
            pl.BlockSpec(memory_space=pltpu.SMEM),
            pl.BlockSpec(bias.shape, lambda b, n: (0, 0, 0, 0)),
            pl.BlockSpec((1, t, SWA_HEADS * HEAD_DIM), lambda b, n: (b, n, q_blk)),
            pl.BlockSpec((1, w, LANES), lambda b, n: (b, prev(n), k_blk)),
            pl.BlockSpec((1, t, LANES), lambda b, n: (b, n, k_blk)),
            pl.BlockSpec((1, w, LANES), lambda b, n: (b, prev(n), v_blk)),
            pl.BlockSpec((1, t, LANES), lambda b, n: (b, n, v_blk)),
        ] + cast_specs,
        out_specs=[pl.BlockSpec((1, t, SWA_HEADS * HEAD_DIM), lambda b, n: (b, n, 0))] + cast_specs,
        out_shape=[jax.ShapeDtypeStruct((B, S, SWA_HEADS * HEAD_DIM), F32)]
        + [jax.ShapeDtypeStruct(weight.shape, BF16) for weight in weights],
        compiler_params=pltpu.CompilerParams(
            dimension_semantics=("parallel", "arbitrary"), vmem_limit_bytes=VMEM_LIMIT),
        name="swa_attn",
    )(sinks, bias, proj, proj, proj, proj, proj, *weights)
    return outs[0], outs[1:]


def _ff_chunks(d_ff):
    n_tiles = d_ff // MXU_TILE
    per = FF_CHUNK // MXU_TILE
    sizes = [per] * (n_tiles // per) + ([n_tiles % per] if n_tiles % per else [])
    return tuple(s * MXU_TILE for s in sizes)


def _post_kernel(ysb_ref, ysw_ref, x_ref, mod_ref, wo_ref, gsb_ref, gsw_ref, l1g_ref, l1b_ref,
                 wgu_ref, wd_ref, l2g_ref, l2b_ref, o_ref, *, alpha, chunks):
    sb_w = ysb_ref.shape[-1]
    d_ff = wd_ref.shape[0]
    d = x_ref.shape[-1]
    batch_row = pl.ds(pl.program_id(0), 1)
    mod = lambda k: mod_ref[batch_row, k * d:(k + 1) * d]

    def attention_tail(rows):
        n_sb = _rms_norm(ysb_ref[0, rows, :], gsb_ref[...]).astype(BF16)
        n_sw = _rms_norm(ysw_ref[0, rows, :], gsw_ref[...]).astype(BF16)
        attn = (jnp.dot(n_sb, wo_ref[:sb_w, :], preferred_element_type=F32)
                + jnp.dot(n_sw, wo_ref[sb_w:, :], preferred_element_type=F32))
        return _layer_norm(alpha * x_ref[0, rows, :] + (1.0 + mod(2)) * attn,
                           l1g_ref[...], l1b_ref[...])

    def swiglu(x1):
        h = (x1 * (1.0 + mod(4)) + mod(3)).astype(BF16)
        ffn = None
        start = 0
        for width in chunks:
            gate = jnp.dot(h, wgu_ref[:, start:start + width], preferred_element_type=F32)
            up = jnp.dot(h, wgu_ref[:, d_ff + start:d_ff + start + width],
                         preferred_element_type=F32)
            a = (gate * jax.nn.sigmoid(gate) * up).astype(BF16)
            part = jnp.dot(a, wd_ref[start:start + width, :], preferred_element_type=F32)
            ffn = part if ffn is None else ffn + part
            start += width
        return ffn

    n_rows = x_ref.shape[1]
    groups = [slice(r, r + n_rows // POST_SPLIT) for r in range(0, n_rows, n_rows // POST_SPLIT)]
    x1s = [attention_tail(rows) for rows in groups]
    ffns = [swiglu(x1) for x1 in x1s]
    for rows, x1, ffn in zip(groups, x1s, ffns):
        o_ref[0, rows, :] = _layer_norm(alpha * x1 + (1.0 + mod(5)) * ffn,
                                        l2g_ref[...], l2b_ref[...])


def _post_attention(y_sb, y_sw, x, mod, w_out, gn_sb, gn_swa, ln1_g, ln1_b, w_gu, w_down,
                    ln2_g, ln2_b, alpha):
    B, S, D = x.shape
    W = y_sb.shape[-1]
    d_ff = w_down.shape[0]
    row = lambda b, i: (b, i, 0)
    const2 = lambda b, i: (0, 0)
    resident = lambda a: pl.BlockSpec(a.shape, const2, pipeline_mode=pl.Buffered(1))
    return pl.pallas_call(
        functools.partial(_post_kernel, alpha=alpha, chunks=_ff_chunks(d_ff)),
        grid=(B, S // POST_ROW_TILE),
        in_specs=[
            pl.BlockSpec((1, POST_ROW_TILE, W), row),
            pl.BlockSpec((1, POST_ROW_TILE, W), row),
            pl.BlockSpec((1, POST_ROW_TILE, D), row),
            resident(mod),
            resident(w_out), resident(gn_sb), resident(gn_swa), resident(ln1_g), resident(ln1_b),
            resident(w_gu), resident(w_down), resident(ln2_g), resident(ln2_b),
        ],
        out_specs=pl.BlockSpec((1, POST_ROW_TILE, D), row),
        out_shape=jax.ShapeDtypeStruct((B, S, D), F32),
        compiler_params=pltpu.CompilerParams(
            dimension_semantics=("parallel", "parallel"), vmem_limit_bytes=VMEM_LIMIT),
        name="post_attn",
    )(y_sb, y_sw, x, mod, w_out, gn_sb, gn_swa, ln1_g, ln1_b, w_gu, w_down, ln2_g, ln2_b)


def kernel(x, c, w_ada, b_ada, w_in, b_in, sinks, gn_sb, gn_swa, w_out, ln1_g, ln1_b,
           w_gu, w_down, ln2_g, ln2_b):
    depth = w_ada.shape[0]
    B, S, D = x.shape
    alpha = (2.0 * depth) ** 0.25
    sb_w = SB_HEADS * HEAD_DIM
    swa_qw = SWA_HEADS * HEAD_DIM
    d_in = w_in.shape[-1]
    cols = np.arange(d_in)
    is_q = (cols < sb_w) | ((cols >= 3 * sb_w) & (cols < 3 * sb_w + swa_qw))
    col_scale = jnp.asarray(np.where(is_q, LOG2_E / math.sqrt(HEAD_DIM), 1.0)[None, :], dtype=F32)
    tri = jnp.asarray(np.arange(SB_TILE)[:, None] >= np.arange(SB_TILE)[None, :], dtype=BF16)
    row2 = lambda a: a.reshape(1, -1)

    for l in range(depth):
        mod = _mod(c, w_ada[l], row2(b_ada[l]))
        proj = _in_proj(x, mod, w_in[l], row2(b_in[l]), col_scale)
        y_sb = _sb_attention(proj, tri)
        y_sw, (w_o, w_g, w_d) = _swa_attention(proj, sinks[l], [w_out[l], w_gu[l], w_down[l]])
        x = _post_attention(y_sb, y_sw, x, mod, w_o, row2(gn_sb[l]), row2(gn_swa[l]),
                            row2(ln1_g[l]), row2(ln1_b[l]), w_g, w_d, row2(ln2_g[l]),
                            row2(ln2_b[l]), alpha)
    return x
```

```python
import functools
import math

import jax
import jax.numpy as jnp
import numpy as np
from jax import lax
from jax.experimental import pallas as pl
from jax.experimental.pallas import tpu as pltpu

F32 = jnp.float32
BF16 = jnp.bfloat16

HEAD_DIM = 64
SB_HEADS = 8
SWA_HEADS = 8
SWA_KV_HEADS = 2
SWA_GROUP = SWA_HEADS // SWA_KV_HEADS
WINDOW = 128
LN_EPS = 1e-5
RMS_EPS = 1e-6
MASK_VALUE = -1e30
EXP2_UNDERFLOW = -150.0
LOG2_E = 1.4426950408889634
LANES = 128
BF16_SUBLANES = 16

SB_TILE = 256
SB_UNROLL = 4
SWA_TILE = 2048
SWA_LAG = 1
ROW_TILE = 1024
POST_ROW_TILE = 1024
POST_SPLIT = 4
MXU_TILE = 256
FF_CHUNK = 1536
MOD_COLS = 1536
VMEM_LIMIT = 60 * 1024 * 1024


def _layer_norm(r, g, b):
    mu = jnp.mean(r, axis=-1, keepdims=True)
    d = r - mu
    var = jnp.mean(d * d, axis=-1, keepdims=True)
    return d * lax.rsqrt(var + LN_EPS) * g + b


def _rms_norm(y, g):
    ms = jnp.mean(y * y, axis=-1, keepdims=True)
    return y * lax.rsqrt(ms + RMS_EPS) * g


def _mod_kernel(ct_ref, wa_ref, wb_ref, b_ref, o_ref):
    ct = ct_ref[...]
    s = ct * jax.nn.sigmoid(ct)
    half = wa_ref.shape[0]
    for b in range(ct.shape[1]):
        col = s[:, b:b + 1]
        o_ref[b:b + 1, :] = (jnp.sum(wa_ref[...] * col[:half], axis=0, keepdims=True)
                             + jnp.sum(wb_ref[...] * col[half:], axis=0, keepdims=True)
                             + b_ref[...])


def _mod(c, w_ada, b_ada):
    B, D = c.shape
    N = w_ada.shape[1]
    return pl.pallas_call(
        _mod_kernel,
        grid=(N // MOD_COLS,),
        in_specs=[
            pl.BlockSpec((D, B), lambda n: (0, 0)),
            pl.BlockSpec((D // 2, MOD_COLS), lambda n: (0, n)),
            pl.BlockSpec((D // 2, MOD_COLS), lambda n: (1, n)),
            pl.BlockSpec((1, MOD_COLS), lambda n: (0, n)),
        ],
        out_specs=pl.BlockSpec((B, MOD_COLS), lambda n: (0, n)),
        out_shape=jax.ShapeDtypeStruct((B, N), F32),
        compiler_params=pltpu.CompilerParams(
            dimension_semantics=("arbitrary",), vmem_limit_bytes=VMEM_LIMIT),
        name="mod",
    )(c.T, w_ada, w_ada, b_ada)


def _inproj_kernel(x_ref, mod_ref, w_ref, b_ref, cs_ref, o_ref, wb_ref, *, col_chunk):
    @pl.when(jnp.logical_and(pl.program_id(0) == 0, pl.program_id(1) == 0))
    def _():
        wb_ref[...] = w_ref[...].astype(BF16)

    d = x_ref.shape[-1]
    batch_row = pl.ds(pl.program_id(0), 1)
    sh = mod_ref[batch_row, 0:d]
    sc = mod_ref[batch_row, d:2 * d]
    h = (x_ref[0] * (1.0 + sc) + sh).astype(BF16)
    n_cols = w_ref.shape[1]
    for n in range(n_cols // col_chunk):
        sl = slice(n * col_chunk, (n + 1) * col_chunk)
        p = jnp.dot(h, wb_ref[:, sl], preferred_element_type=F32)
        o_ref[0, :, sl] = ((p + b_ref[:, sl]) * cs_ref[:, sl]).astype(BF16)


def _in_proj(x, mod, w_in, b_in, col_scale):
    B, S, D = x.shape
    N = w_in.shape[1]
    return pl.pallas_call(
        functools.partial(_inproj_kernel, col_chunk=768),
        grid=(B, S // ROW_TILE),
        in_specs=[
            pl.BlockSpec((1, ROW_TILE, D), lambda b, i: (b, i, 0)),
            pl.BlockSpec(mod.shape, lambda b, i: (0, 0)),
            pl.BlockSpec((D, N), lambda b, i: (0, 0), pipeline_mode=pl.Buffered(1)),
            pl.BlockSpec((1, N), lambda b, i: (0, 0)),
            pl.BlockSpec((1, N), lambda b, i: (0, 0)),
        ],
        out_specs=pl.BlockSpec((1, ROW_TILE, N), lambda b, i: (b, i, 0)),
        out_shape=jax.ShapeDtypeStruct((B, S, N), BF16),
        scratch_shapes=[pltpu.VMEM((D, N), BF16)],
        compiler_params=pltpu.CompilerParams(
            dimension_semantics=("arbitrary", "arbitrary"), vmem_limit_bytes=VMEM_LIMIT),
        name="in_proj",
    )(x, mod, w_in, b_in, col_scale)


def _sb_kernel(q_ref, k_ref, v_ref, u_ref, o_ref):
    t = SB_TILE
    n_tiles = q_ref.shape[1] // t
    u = u_ref[...]
    first_head = lambda rows: lax.broadcasted_iota(jnp.int32, (rows, LANES), 1) < HEAD_DIM
    row = lax.broadcasted_iota(jnp.int32, (2 * t, t), 0)
    col = lax.broadcasted_iota(jnp.int32, (2 * t, t), 1)
    before = col < (row & (t - 1))
    nt_dims = (((1,), (1,)), ((), ()))

    def log_one_minus_sigmoid(z):
        neg_part = jnp.minimum(z, 0.0)
        neg_relu = neg_part - z
        return neg_relu - jnp.log2(1.0 + jnp.exp2(neg_part + neg_relu))

    def suffix(lr):
        return jnp.dot(lr.astype(BF16), u, preferred_element_type=F32)

    def row_total(lr):
        return jnp.sum(lr, axis=-1, keepdims=True)

    def weighted_values(w, vals):
        w = w.astype(BF16)
        zero = jnp.zeros_like(vals)
        in_first = first_head(vals.shape[0])
        w2 = jnp.concatenate([w[:t], w[t:]], axis=1)
        v2 = jnp.concatenate([jnp.where(in_first, vals, zero), jnp.where(in_first, zero, vals)],
                             axis=0)
        return jnp.dot(w2, v2, preferred_element_type=F32)

    def score_diagonal_and_previous(i):
        here = pl.multiple_of(i * t, t)
        prev = pl.multiple_of(jnp.maximum(i - 1, 0) * t, t)
        q = q_ref[0, pl.ds(here, t), :]
        zero = jnp.zeros_like(q)
        in_first = first_head(t)
        qs = jnp.concatenate([jnp.where(in_first, q, zero), jnp.where(in_first, zero, q)], axis=0)

        keys = jnp.concatenate([k_ref[0, pl.ds(prev, t), :], k_ref[0, pl.ds(here, t), :]], axis=0)
        vals = jnp.concatenate([v_ref[0, pl.ds(prev, t), :], v_ref[0, pl.ds(here, t), :]], axis=0)
        z = lax.dot_general(qs, keys, nt_dims, preferred_element_type=F32)
        return qs, vals, z, log_one_minus_sigmoid(z)

    def weigh_diagonal_and_previous(i, qs, vals, z, lr):
        lr_p, lr_d = lr[:, :t], jnp.where(before, lr[:, t:], 0.0)
        run_p = row_total(lr_d) + jnp.where(i == 0, MASK_VALUE, 0.0)
        w_d = jnp.where(before, jnp.exp2(z[:, t:] + suffix(lr_d)), 0.0)
        w_p = jnp.exp2(z[:, :t] + suffix(lr_p) + run_p)
        acc = weighted_values(jnp.concatenate([w_p, w_d], axis=1), vals)
        run = run_p + row_total(lr_p)
        return qs, run, acc, jnp.max(run) >= EXP2_UNDERFLOW

    def earlier_tiles(i, qs, run, acc, live):
        def cond(c):
            j, live, _, _ = c
            return jnp.logical_and(j >= 0, live)

        def body(c):
            j, _, run, acc = c
            start = pl.multiple_of(j * t, t)
            z = lax.dot_general(qs, k_ref[0, pl.ds(start, t), :], nt_dims,
                                preferred_element_type=F32)
            lr = log_one_minus_sigmoid(z)
            w = jnp.exp2(z + suffix(lr) + run)
            acc = acc + weighted_values(w, v_ref[0, pl.ds(start, t), :])
            run = run + row_total(lr)
            return j - 1, jnp.max(run) >= EXP2_UNDERFLOW, run, acc

        _, _, _, acc = lax.while_loop(cond, body, (i - 2, live, run, acc))
        o_ref[0, pl.ds(pl.multiple_of(i * t, t), t), :] = acc

    def q_tiles(g, carry):
        tiles = [g * SB_UNROLL + s for s in range(SB_UNROLL)]
        heads, scored = [], score_diagonal_and_previous(tiles[0])
        for s, i in enumerate(tiles):
            ahead = score_diagonal_and_previous(tiles[s + 1]) if s + 1 < len(tiles) else None
            heads.append(weigh_diagonal_and_previous(i, *scored))
            scored = ahead
        for i, head in zip(tiles, heads):
            earlier_tiles(i, *head)
        return carry

    lax.fori_loop(0, n_tiles // SB_UNROLL, q_tiles, 0)


def _sb_attention(proj, u):
    B, S, _ = proj.shape
    n_pairs = SB_HEADS * HEAD_DIM // LANES
    seq = lambda blk: pl.BlockSpec((1, S, LANES), lambda b, p: (b, 0, blk * n_pairs + p))
    return pl.pallas_call(
        _sb_kernel,
        grid=(B, n_pairs),
        in_specs=[seq(0), seq(1), seq(2), pl.BlockSpec(u.shape, lambda b, p: (0, 0))],
        out_specs=pl.BlockSpec((1, S, LANES), lambda b, p: (b, 0, p)),
        out_shape=jax.ShapeDtypeStruct((B, S, SB_HEADS * HEAD_DIM), F32),
        compiler_params=pltpu.CompilerParams(
            dimension_semantics=("parallel", "parallel"), vmem_limit_bytes=VMEM_LIMIT),
        name="sb_attn",
    )(proj, proj, proj, u)


def _swa_bias():
    w = WINDOW
    r = np.arange(2 * w)[:, None]
    c = np.arange(4 * w)[None, :]
    dist = (r % w) + w - (c % (2 * w))
    in_band = (dist >= 0) & (dist < w)
    groups = []
    for g in range(SWA_KV_HEADS):
        head = SWA_GROUP * g + 2 * (r // w) + (c // (2 * w))
        slope = np.exp2(-8.0 * (head + 1) / SWA_HEADS)
        groups.append(np.where(in_band, -slope * dist * LOG2_E, MASK_VALUE))
    rest = np.stack(groups)
    first = np.where((c % (2 * w)) >= w, rest, MASK_VALUE)
    return jnp.asarray(np.stack([first, rest]), dtype=F32)


def _swa_kernel(sink_ref, bias_ref, q_ref, kp_ref, kc_ref, vp_ref, vc_ref, *rest):
    w = WINDOW
    n_cast = (len(rest) - 1) // 2
    o_ref = rest[n_cast]
    n = pl.program_id(1)
    lane = lax.broadcasted_iota(jnp.int32, (w, LANES), 1)
    lo = lane < HEAD_DIM
    nt_dims = (((1,), (1,)), ((), ()))

    def placed(prev_ref, cur_ref):
        band = jnp.concatenate([prev_ref[0], cur_ref[0]], axis=0).astype(F32)
        swapped = pltpu.roll(band, HEAD_DIM, axis=1)
        in_lo = lax.broadcasted_iota(jnp.int32, band.shape, 1) < HEAD_DIM
        return [
            (jnp.where(in_lo, band, 0.0).astype(BF16), jnp.where(in_lo, 0.0, swapped).astype(BF16)),
            (jnp.where(in_lo, swapped, 0.0).astype(BF16), jnp.where(in_lo, 0.0, band).astype(BF16)),
        ]

    k_placed = placed(kp_ref, kc_ref)
    v_placed = placed(vp_ref, vc_ref)
    band_lo = lax.broadcasted_iota(jnp.int32, (2 * w, LANES), 1) < HEAD_DIM
    ones_lo = jnp.where(band_lo, 1.0, 0.0).astype(BF16)
    ones_hi = jnp.where(band_lo, 0.0, 1.0).astype(BF16)
    variant = jnp.minimum(n, 1)

    def scores(a, g):
        rows = slice(a * w, (a + 1) * w)
        band = slice(a * w, (a + 2) * w)
        cols = [slice((2 * g + rb) * LANES, (2 * g + rb + 1) * LANES) for rb in range(2)]
        q2 = jnp.concatenate([q_ref[0, rows, cols[0]], q_ref[0, rows, cols[1]]], axis=0)
        k2 = jnp.concatenate([k_placed[g][0][band], k_placed[g][1][band]], axis=0)
        bias = bias_ref[variant, g] if a == 0 else bias_ref[1, g]
        return lax.dot_general(q2, k2, nt_dims, preferred_element_type=F32) + bias

    def softmax_numerators(g, s):
        p_rows, sink_terms = [], []
        for rb in range(2):
            p_cols, sink_cols = [], []
            for half in range(2):
                sink = sink_ref[SWA_GROUP * g + 2 * rb + half] * LOG2_E
                sq = s[rb * w:(rb + 1) * w, half * 2 * w:(half + 1) * 2 * w]
                m = jnp.maximum(jnp.max(sq, axis=-1, keepdims=True), sink)
                p_cols.append(jnp.exp2(sq - m).astype(BF16))
                sink_cols.append(jnp.exp2(sink - m))
            p_rows.append(jnp.concatenate(p_cols, axis=1))
            sink_terms.append(jnp.where(lo, sink_cols[0], sink_cols[1]))
        return jnp.concatenate(p_rows, axis=0), sink_terms

    def values(a, g, p, sink_terms):
        rows = slice(a * w, (a + 1) * w)
        band = slice(a * w, (a + 2) * w)
        v2 = jnp.concatenate([jnp.concatenate([v_placed[g][0][band], ones_lo], axis=1),
                              jnp.concatenate([v_placed[g][1][band], ones_hi], axis=1)], axis=0)
        o2 = jnp.dot(p, v2, preferred_element_type=F32)
        for rb in range(2):
            sub = slice(rb * w, (rb + 1) * w)
            cols = slice((2 * g + rb) * LANES, (2 * g + rb + 1) * LANES)
            o_ref[0, rows, cols] = o2[sub, :LANES] / (o2[sub, LANES:] + sink_terms[rb])

    items = [(a, g) for a in range(SWA_TILE // w) for g in range(SWA_KV_HEADS)]
    scored, normed = {}, {}
    for step in range(len(items) + 2 * SWA_LAG):
        if step < len(items):
            scored[step] = scores(*items[step])
        mid, last = step - SWA_LAG, step - 2 * SWA_LAG
        if 0 <= mid < len(items):
            normed[mid] = softmax_numerators(items[mid][1], scored.pop(mid))
        if 0 <= last < len(items):
            values(*items[last], *normed.pop(last))

    for src_ref, dst_ref in zip(rest[:n_cast], rest[n_cast + 1:]):
        dst_ref[...] = src_ref[...].astype(BF16)


def _swa_attention(proj, sinks, weights):
    B, S, _ = proj.shape
    t = SWA_TILE
    w = WINDOW
    n_steps = B * (S // t)
    q_blk = (3 * SB_HEADS * HEAD_DIM) // (SWA_HEADS * HEAD_DIM)
    k_blk = (3 * SB_HEADS + SWA_HEADS) * HEAD_DIM // LANES
    v_blk = k_blk + 1
    prev = lambda n: jnp.maximum(n * (t // w) - 1, 0)
    bias = _swa_bias()

    def row_slice(weight):
        rows = weight.shape[0] // n_steps
        assert rows * n_steps == weight.shape[0] and rows % BF16_SUBLANES == 0
        return pl.BlockSpec((rows, weight.shape[1]), lambda b, n: (b * (S // t) + n, 0))

    cast_specs = [row_slice(weight) for weight in weights]
    outs = pl.pallas_call(
        _swa_kernel,
        grid=(B, S // t),
        in_specs=[
            pl.BlockSpec(memory_space=pltpu.SMEM),
            pl.BlockSpec(bias.shape, lambda b, n: (0, 0, 0, 0)),
            pl.BlockSpec((1, t, SWA_HEADS * HEAD_DIM), lambda b, n: (b, n, q_blk)),
            pl.BlockSpec((1, w, LANES), lambda b, n: (b, prev(n), k_blk)),
            pl.BlockSpec((1, t, LANES), lambda b, n: (b, n, k_blk)),
            pl.BlockSpec((1, w, LANES), lambda b, n: (b, prev(n), v_blk)),
            pl.BlockSpec((1, t, LANES), lambda b, n: (b, n, v_blk)),
        ] + cast_specs,
        out_specs=[pl.BlockSpec((1, t, SWA_HEADS * HEAD_DIM), lambda b, n: (b, n, 0))] + cast_specs,
        out_shape=[jax.ShapeDtypeStruct((B, S, SWA_HEADS * HEAD_DIM), F32)]
        + [jax.ShapeDtypeStruct(weight.shape, BF16) for weight in weights],
        compiler_params=pltpu.CompilerParams(
            dimension_semantics=("parallel", "arbitrary"), vmem_limit_bytes=VMEM_LIMIT),
        name="swa_attn",
    )(sinks, bias, proj, proj, proj, proj, proj, *weights)
    return outs[0], outs[1:]


def _ff_chunks(d_ff):
    n_tiles = d_ff // MXU_TILE
    per = FF_CHUNK // MXU_TILE
    sizes = [per] * (n_tiles // per) + ([n_tiles % per] if n_tiles % per else [])
    return tuple(s * MXU_TILE for s in sizes)


def _post_kernel(ysb_ref, ysw_ref, x_ref, mod_ref, wo_ref, gsb_ref, gsw_ref, l1g_ref, l1b_ref,
                 wgu_ref, wd_ref, l2g_ref, l2b_ref, o_ref, *, alpha, chunks):
    sb_w = ysb_ref.shape[-1]
    d_ff = wd_ref.shape[0]
    d = x_ref.shape[-1]
    batch_row = pl.ds(pl.program_id(0), 1)
    mod = lambda k: mod_ref[batch_row, k * d:(k + 1) * d]

    def attention_tail(rows):
        n_sb = _rms_norm(ysb_ref[0, rows, :], gsb_ref[...]).astype(BF16)
        n_sw = _rms_norm(ysw_ref[0, rows, :], gsw_ref[...]).astype(BF16)
        attn = (jnp.dot(n_sb, wo_ref[:sb_w, :], preferred_element_type=F32)
                + jnp.dot(n_sw, wo_ref[sb_w:, :], preferred_element_type=F32))
        return _layer_norm(alpha * x_ref[0, rows, :] + (1.0 + mod(2)) * attn,
                           l1g_ref[...], l1b_ref[...])

    def swiglu(x1):
        h = (x1 * (1.0 + mod(4)) + mod(3)).astype(BF16)
        ffn = None
        start = 0
        for width in chunks:
            gate = jnp.dot(h, wgu_ref[:, start:start + width], preferred_element_type=F32)
            up = jnp.dot(h, wgu_ref[:, d_ff + start:d_ff + start + width],
                         preferred_element_type=F32)
            a = (gate * jax.nn.sigmoid(gate) * up).astype(BF16)
            part = jnp.dot(a, wd_ref[start:start + width, :], preferred_element_type=F32)
            ffn = part if ffn is None else ffn + part
            start += width
        return ffn

    n_rows = x_ref.shape[1]
    groups = [slice(r, r + n_rows // POST_SPLIT) for r in range(0, n_rows, n_rows // POST_SPLIT)]
    x1s = [attention_tail(rows) for rows in groups]
    ffns = [swiglu(x1) for x1 in x1s]
    for rows, x1, ffn in zip(groups, x1s, ffns):
        o_ref[0, rows, :] = _layer_norm(alpha * x1 + (1.0 + mod(5)) * ffn,
                                        l2g_ref[...], l2b_ref[...])


def _post_attention(y_sb, y_sw, x, mod, w_out, gn_sb, gn_swa, ln1_g, ln1_b, w_gu, w_down,
                    ln2_g, ln2_b, alpha):
    B, S, D = x.shape
    W = y_sb.shape[-1]
    d_ff = w_down.shape[0]
    row = lambda b, i: (b, i, 0)
    const2 = lambda b, i: (0, 0)
    resident = lambda a: pl.BlockSpec(a.shape, const2, pipeline_mode=pl.Buffered(1))
    return pl.pallas_call(
        functools.partial(_post_kernel, alpha=alpha, chunks=_ff_chunks(d_ff)),
        grid=(B, S // POST_ROW_TILE),
        in_specs=[
            pl.BlockSpec((1, POST_ROW_TILE, W), row),
            pl.BlockSpec((1, POST_ROW_TILE, W), row),
            pl.BlockSpec((1, POST_ROW_TILE, D), row),
            resident(mod),
            resident(w_out), resident(gn_sb), resident(gn_swa), resident(ln1_g), resident(ln1_b),
            resident(w_gu), resident(w_down), resident(ln2_g), resident(ln2_b),
        ],
        out_specs=pl.BlockSpec((1, POST_ROW_TILE, D), row),
        out_shape=jax.ShapeDtypeStruct((B, S, D), F32),
        compiler_params=pltpu.CompilerParams(
            dimension_semantics=("parallel", "parallel"), vmem_limit_bytes=VMEM_LIMIT),
        name="post_attn",
    )(y_sb, y_sw, x, mod, w_out, gn_sb, gn_swa, ln1_g, ln1_b, w_gu, w_down, ln2_g, ln2_b)


def kernel(x, c, w_ada, b_ada, w_in, b_in, sinks, gn_sb, gn_swa, w_out, ln1_g, ln1_b,
           w_gu, w_down, ln2_g, ln2_b):
    depth = w_ada.shape[0]
    B, S, D = x.shape
    alpha = (2.0 * depth) ** 0.25
    sb_w = SB_HEADS * HEAD_DIM
    swa_qw = SWA_HEADS * HEAD_DIM
    d_in = w_in.shape[-1]
    cols = np.arange(d_in)
    is_q = (cols < sb_w) | ((cols >= 3 * sb_w) & (cols < 3 * sb_w + swa_qw))
    col_scale = jnp.asarray(np.where(is_q, LOG2_E / math.sqrt(HEAD_DIM), 1.0)[None, :], dtype=F32)
    tri = jnp.asarray(np.arange(SB_TILE)[:, None] >= np.arange(SB_TILE)[None, :], dtype=BF16)
    row2 = lambda a: a.reshape(1, -1)

    for l in range(depth):
        mod = _mod(c, w_ada[l], row2(b_ada[l]))
        proj = _in_proj(x, mod, w_in[l], row2(b_in[l]), col_scale)
        y_sb = _sb_attention(proj, tri)
        y_sw, (w_o, w_g, w_d) = _swa_attention(proj, sinks[l], [w_out[l], w_gu[l], w_down[l]])
        x = _post_attention(y_sb, y_sw, x, mod, w_o, row2(gn_sb[l]), row2(gn_swa[l]),
                            row2(ln1_g[l]), row2(ln1_b[l]), w_g, w_d, row2(ln2_g[l]),
                            row2(ln2_b[l]), alpha)
    return x
```
